```python
import jax, jax.numpy as jnp
from jax import lax
import numpy as np

D_MODEL = 1024
BATCH = 16
SEQ = 2048
DEPTH = 1

CHUNK = 64
Q_BLOCK = 128
ROPE_THETA = 10000.0
EPS = 1e-6
GN_EPS = 1e-5
NEG_BIG = -1e30
A_HEADS = 8
A_HEAD_DIM = 64
IDX_HEADS = 8
IDX_DIM = 64
TOPK_MAX = 256
A_WIDTH = A_HEADS * A_HEAD_DIM
R_HEADS = 4
R_QK_DIM = 128
R_V_DIM = 256
R_WIDTH = R_HEADS * R_V_DIM
D_FF = 2816
PLE_DIM = 256
IN_SPLITS = (
    A_HEADS * A_HEAD_DIM,
    A_HEAD_DIM,
    A_HEAD_DIM,
    IDX_HEADS * IDX_DIM,
    IDX_DIM,
    IDX_HEADS,
    R_HEADS * R_QK_DIM,
    R_HEADS * R_QK_DIM,
    R_HEADS * R_V_DIM,
    R_HEADS * R_V_DIM,
    D_MODEL,
    D_MODEL,
)
IN_WIDTH = sum(IN_SPLITS)

kernel_name = 'hybrid_dsa_retention_macaron'


def rms_norm(x, g):
    xf = x.astype(jnp.float32)
    y = xf * lax.rsqrt(jnp.mean(xf * xf, axis=-1, keepdims=True) + EPS)
    return (y * g.astype(jnp.float32)).astype(x.dtype)


def swiglu(x, w_gate, w_up, w_down):
    return (jax.nn.silu(x @ w_gate) * (x @ w_up)) @ w_down


def rope(t, pos):
    half = t.shape[-1] // 2
    inv = ROPE_THETA ** (-jnp.arange(half, dtype=jnp.float32) / half)
    ang = pos.astype(jnp.float32)[..., None] * inv
    cos = jnp.cos(ang)[:, :, None, :]
    sin = jnp.sin(ang)[:, :, None, :]
    t1 = t[..., :half].astype(jnp.float32)
    t2 = t[..., half:].astype(jnp.float32)
    return jnp.concatenate([t1 * cos - t2 * sin, t2 * cos + t1 * sin], axis=-1).astype(t.dtype)


def dsa_attention(q, k, v, q_idx, k_idx, w_idx, n_sel):
    b, s = q.shape[0], q.shape[1]
    nb = s // Q_BLOCK
    key_pos = jnp.arange(s)
    k_idx32 = k_idx.astype(jnp.float32)

    def to_blocks(t):
        return jnp.moveaxis(t.reshape((b, nb, Q_BLOCK) + t.shape[2:]), 1, 0)

    def one_block(args):
        qb, qib, wb, blk = args
        qpos = blk * Q_BLOCK + jnp.arange(Q_BLOCK)
        visible_end = (qpos // CHUNK + 1) * CHUNK
        admissible = key_pos[None, :] < visible_end[:, None]
        rel = jax.nn.relu(jnp.einsum('bqhd,bsd->bqhs', qib.astype(jnp.float32), k_idx32))
        score = jnp.einsum('bqhs,bqh->bqs', rel, wb.astype(jnp.float32))
        score = jnp.where(admissible[None], score, NEG_BIG)
        top_val, top_pos = lax.top_k(score, n_sel)
        valid = top_val > 0.5 * NEG_BIG
        k_sel = jax.vmap(lambda kb, ib: kb[ib])(k, top_pos)
        v_sel = jax.vmap(lambda vb, ib: vb[ib])(v, top_pos)
        logits = jnp.einsum('bqhd,bqkd->bqhk', qb, k_sel).astype(jnp.float32) * (A_HEAD_DIM ** -0.5)
        logits = jnp.where(valid[:, :, None, :], logits, NEG_BIG)
        probs = jax.nn.softmax(logits, axis=-1).astype(v.dtype)
        return jnp.einsum('bqhk,bqkd->bqhd', probs, v_sel)

    out = lax.map(one_block, (to_blocks(q), to_blocks(q_idx), to_blocks(w_idx), jnp.arange(nb)))
    return jnp.moveaxis(out, 0, 1).reshape(b, s, -1)


def retention(q, k, v):
    b, s, h, dk = q.shape
    dv = v.shape[-1]
    n = s // CHUNK
    log_gamma = jnp.log1p(-jnp.exp2(-5.0 - jnp.arange(h, dtype=jnp.float32)))
    idx = jnp.arange(CHUNK, dtype=jnp.float32)
    decay_intra = jnp.exp(log_gamma[:, None, None] * jnp.abs(idx[:, None] - idx[None, :]))
    decay_q = jnp.exp(log_gamma[None, :] * (idx[:, None] + 1.0))
    decay_k = jnp.exp(log_gamma[None, :] * (CHUNK - 1.0 - idx[:, None]))
    decay_chunk = jnp.exp(log_gamma * CHUNK)
    qc = q.reshape(b, n, CHUNK, h, dk)
    kc = (k * (dk ** -0.5)).reshape(b, n, CHUNK, h, dk)
    vc = v.reshape(b, n, CHUNK, h, dv)
    scores = jnp.einsum('bnihd,bnjhd->bnhij', qc, kc) * decay_intra
    y_intra = jnp.einsum('bnhij,bnjhe->bnihe', scores, vc)
    kv = jnp.einsum('bnjhd,bnjhe,jh->nbhde', kc, vc, decay_k)

    def step(state, kv_n):
        return decay_chunk[None, :, None, None] * state + kv_n, state

    _, prev = lax.scan(step, jnp.zeros((b, h, dk, dv), jnp.float32), kv)
    y_cross = jnp.einsum('bnihd,nbhde->bnihe', qc, prev) * decay_q[None, None, :, :, None]
    return (y_intra + y_cross).reshape(b, s, h, dv)


def group_norm_heads(y, g):
    mean = jnp.mean(y, axis=-1, keepdims=True)
    var = jnp.mean(jnp.square(y - mean), axis=-1, keepdims=True)
    yn = (y - mean) * lax.rsqrt(var + GN_EPS)
    return yn.reshape(y.shape[0], y.shape[1], -1) * g.astype(jnp.float32)


def setup_inputs(seed: int = 0) -> dict:
    key = jax.random.key(seed)
    ks = jax.random.split(key, 24)
    f32 = jnp.float32

    def w(k, shape, fan_in, scale=1.0):
        return jax.random.normal(k, shape, f32) * (scale * fan_in ** -0.5)

    def gain(k, shape):
        return 1.0 + 0.05 * jax.random.normal(k, shape, f32)

    offsets = jax.random.randint(ks[2], (BATCH, 1), 0, 4096)
    positions = (offsets + jnp.arange(SEQ)[None, :]).astype(jnp.int32)
    return {
        'x': jax.random.normal(ks[0], (BATCH, SEQ, D_MODEL), f32),
        'p': jax.random.normal(ks[1], (DEPTH, BATCH, SEQ, PLE_DIM), f32),
        'positions': positions,
        'ffn1_norm': gain(ks[3], (DEPTH, D_MODEL)),
        'ffn1_w_gate': w(ks[4], (DEPTH, D_MODEL, D_FF), D_MODEL),
        'ffn1_w_up': w(ks[5], (DEPTH, D_MODEL, D_FF), D_MODEL),
        'ffn1_w_down': w(ks[6], (DEPTH, D_FF, D_MODEL), D_FF, 0.5),
        'mix_norm': gain(ks[7], (DEPTH, D_MODEL)),
        'w_in': w(ks[8], (DEPTH, D_MODEL, IN_WIDTH), D_MODEL),
        'ret_gn': gain(ks[9], (DEPTH, R_WIDTH)),
        'w_branch_a': w(ks[10], (DEPTH, A_WIDTH, D_MODEL), A_WIDTH),
        'w_branch_b': w(ks[11], (DEPTH, R_WIDTH, D_MODEL), R_WIDTH),
        'w_out': w(ks[12], (DEPTH, D_MODEL, D_MODEL), D_MODEL, 0.5),
        'ffn2_norm': gain(ks[13], (DEPTH, D_MODEL)),
        'ffn2_w_gate': w(ks[14], (DEPTH, D_MODEL, D_FF), D_MODEL),
        'ffn2_w_up': w(ks[15], (DEPTH, D_MODEL, D_FF), D_MODEL),
        'ffn2_w_down': w(ks[16], (DEPTH, D_FF, D_MODEL), D_FF, 0.5),
        'ple_norm': gain(ks[17], (DEPTH, D_MODEL)),
        'w_ple_gate': w(ks[18], (DEPTH, D_MODEL, D_MODEL), D_MODEL),
        'w_ple_proj': w(ks[19], (DEPTH, PLE_DIM, D_MODEL), PLE_DIM, 0.5),
        'final_norm': gain(ks[20], (D_MODEL,)),
    }


def reference(x, p, positions, ffn1_norm, ffn1_w_gate, ffn1_w_up, ffn1_w_down,
              mix_norm, w_in, ret_gn, w_branch_a, w_branch_b, w_out,
              ffn2_norm, ffn2_w_gate, ffn2_w_up, ffn2_w_down,
              ple_norm, w_ple_gate, w_ple_proj, final_norm):
    b, s, _ = x.shape
    n_sel = min(TOPK_MAX, s // 4)
    split_points = []
    acc = 0
    for sz in IN_SPLITS[:-1]:
        acc += sz
        split_points.append(acc)

    h = x
    for i in range(DEPTH):
        h = h + 0.5 * swiglu(rms_norm(h, ffn1_norm[i]), ffn1_w_gate[i], ffn1_w_up[i], ffn1_w_down[i])

        u = rms_norm(h, mix_norm[i])
        z = u @ w_in[i]
        (aq, ak, av, iq, ik, iw, rq, rk, rv, rg, ga, gb) = jnp.split(z, split_points, axis=-1)

        aq = rope(aq.reshape(b, s, A_HEADS, A_HEAD_DIM), positions)
        ak = rope(ak[:, :, None, :], positions)[:, :, 0, :]
        iq = rope(iq.reshape(b, s, IDX_HEADS, IDX_DIM), positions) * (IDX_DIM ** -0.5)
        ik = rope(ik[:, :, None, :], positions)[:, :, 0, :]
        iw = iw * (IDX_HEADS ** -0.5)
        y_a = dsa_attention(aq, ak, av, iq, ik, iw, n_sel)

        rq = rope(rq.reshape(b, s, R_HEADS, R_QK_DIM), positions).astype(jnp.float32)
        rk = rope(rk.reshape(b, s, R_HEADS, R_QK_DIM), positions).astype(jnp.float32)
        rv = rv.reshape(b, s, R_HEADS, R_V_DIM).astype(jnp.float32)
        y_r = group_norm_heads(retention(rq, rk, rv), ret_gn[i]).astype(x.dtype) * jax.nn.silu(rg)

        merged = jax.nn.sigmoid(ga) * (y_a @ w_branch_a[i]) + jax.nn.sigmoid(gb) * (y_r @ w_branch_b[i])
        h = h + merged @ w_out[i]

        h = h + 0.5 * swiglu(rms_norm(h, ffn2_norm[i]), ffn2_w_gate[i], ffn2_w_up[i], ffn2_w_down[i])

        gate = jax.nn.sigmoid(rms_norm(h, ple_norm[i]) @ w_ple_gate[i])
        h = h + gate * (p[i] @ w_ple_proj[i])

    return rms_norm(h, final_norm)
```

```python
import functools
import math

import jax
import jax.numpy as jnp
import numpy as np
from jax import lax
from jax.experimental import pallas as pl
from jax.experimental.pallas import tpu as pltpu

F32 = jnp.float32
BF16 = jnp.bfloat16
I32 = jnp.int32

D_MODEL = 1024
CHUNK = 64
Q_BLOCK = 128
ROPE_THETA = 10000.0
EPS = 1e-6
GN_EPS = 1e-5
NEG_BIG = -1e30
A_HEADS = 8
A_HEAD_DIM = 64
IDX_HEADS = 8
IDX_DIM = 64
TOPK_MAX = 256
A_WIDTH = A_HEADS * A_HEAD_DIM
R_HEADS = 4
R_QK_DIM = 128
R_V_DIM = 256
R_WIDTH = R_HEADS * R_V_DIM
D_FF = 2816
PLE_DIM = 256

_OFF_AQ = 0
_OFF_AK = _OFF_AQ + A_WIDTH
_OFF_AV = _OFF_AK + A_HEAD_DIM
_OFF_IQ = _OFF_AV + A_HEAD_DIM
_OFF_IK = _OFF_IQ + IDX_HEADS * IDX_DIM
_OFF_IW = _OFF_IK + IDX_DIM
_OFF_RQ = _OFF_IW + IDX_HEADS
_OFF_RK = _OFF_RQ + R_HEADS * R_QK_DIM
_OFF_RV = _OFF_RK + R_HEADS * R_QK_DIM
_N_TOKEN_MAJOR = 2 * R_WIDTH + 2 * D_MODEL

_T_AQ = 0
_T_IQ = _T_AQ + A_WIDTH
_T_RQ = _T_IQ + IDX_HEADS * IDX_DIM
_T_RK = _T_RQ + R_HEADS * R_QK_DIM
_T_AK = _T_RK + R_HEADS * R_QK_DIM
_T_IK = _T_AK + A_HEAD_DIM
_T_AV = _T_IK + IDX_DIM
_T_IW = _T_AV + A_HEAD_DIM
_T_ROWS = _T_IW + IDX_HEADS

VMEM_LIMIT = 56 * 1024 * 1024
FF_CHUNK = 256
KEY_TILE = 256
RET_BLOCK = 256

INT_MIN = -(2 ** 31)


def _sortable_key_of(value):
    bits = int(np.float32(value).view(np.int32))
    return bits ^ ((bits >> 31) & 0x7FFFFFFF)


KEY_INVALID = _sortable_key_of(0.5 * NEG_BIG)


def _const_spec(shape):
    nd = len(shape)
    return pl.BlockSpec(shape, lambda *_: (0,) * nd, pipeline_mode=pl.Buffered(1))


def _rms(x, g):
    ms = jnp.mean(x * x, axis=-1, keepdims=True)
    return x * lax.rsqrt(ms + EPS) * g


def _dot(a, b):
    return jnp.dot(a, b, preferred_element_type=F32)


def _ffn_kernel(h_ref, g_ref, wg_ref, wu_ref, wd_ref, o_ref):
    h = h_ref[...]
    u = _rms(h, g_ref[...]).astype(BF16)
    acc = jnp.zeros(h.shape, F32)
    for c in range(D_FF // FF_CHUNK):
        sl = slice(c * FF_CHUNK, (c + 1) * FF_CHUNK)
        gate = _dot(u, wg_ref[:, sl])
        up = _dot(u, wu_ref[:, sl])
        act = (gate * jax.nn.sigmoid(gate) * up).astype(BF16)
        acc = acc + _dot(act, wd_ref[sl, :])
    o_ref[...] = h + 0.5 * acc


def _ffn(h, g, wg, wu, wd, tm):
    t = h.shape[0]
    return pl.pallas_call(
        _ffn_kernel,
        grid=(t // tm,),
        in_specs=[
            pl.BlockSpec((tm, D_MODEL), lambda i: (i, 0)),
            _const_spec((1, D_MODEL)),
            _const_spec((D_MODEL, D_FF)),
            _const_spec((D_MODEL, D_FF)),
            _const_spec((D_FF, D_MODEL)),
        ],
        out_specs=pl.BlockSpec((tm, D_MODEL), lambda i: (i, 0)),
        out_shape=jax.ShapeDtypeStruct((t, D_MODEL), F32),
        compiler_params=pltpu.CompilerParams(
            dimension_semantics=("parallel",), vmem_limit_bytes=VMEM_LIMIT),
        name="ffn",
    )(h, g, wg, wu, wd)


def _rope_rows(t1, t2, cos, sin):
    return t1 * cos - t2 * sin, t2 * cos + t1 * sin


def _inproj_kernel(h_ref, g_ref, pos_ref, wt_ref, wn_ref,
                   aqt_ref, iqt_ref, rq_ref, rkt_ref, kk_ref, avt_ref, iwt_ref,
                   rv_ref, rg_ref, ga_ref, gb_ref):
    tm = h_ref.shape[0]
    u = _rms(h_ref[...], g_ref[...]).astype(BF16)

    zn = _dot(u, wn_ref[...])
    rv_ref[...] = zn[:, 0:R_WIDTH].astype(BF16)
    rg_ref[...] = zn[:, R_WIDTH:2 * R_WIDTH].astype(BF16)
    ga_ref[...] = zn[:, 2 * R_WIDTH:2 * R_WIDTH + D_MODEL].astype(BF16)
    gb_ref[...] = zn[:, 2 * R_WIDTH + D_MODEL:].astype(BF16)

    zt = lax.dot_general(wt_ref[...], u, (((1,), (1,)), ((), ())),
                         preferred_element_type=F32)

    pos = pos_ref[0].astype(F32)

    def angles(half):
        i = lax.broadcasted_iota(I32, (half, tm), 0).astype(F32)
        inv = jnp.exp(i * (-math.log(ROPE_THETA) / half))
        ang = pos * inv
        return jnp.cos(ang), jnp.sin(ang)

    cos64, sin64 = angles(A_HEAD_DIM // 2)
    cos128, sin128 = angles(R_QK_DIM // 2)

    def rope_head(row0, dim, cos, sin):
        half = dim // 2
        return _rope_rows(zt[row0:row0 + half], zt[row0 + half:row0 + dim], cos, sin)

    for h in range(A_HEADS):
        o1, o2 = rope_head(_T_AQ + h * A_HEAD_DIM, A_HEAD_DIM, cos64, sin64)
        r = h * A_HEAD_DIM
        aqt_ref[r:r + 32, :] = (o1 * (A_HEAD_DIM ** -0.5)).astype(BF16)
        aqt_ref[r + 32:r + 64, :] = (o2 * (A_HEAD_DIM ** -0.5)).astype(BF16)
    for h in range(IDX_HEADS):
        o1, o2 = rope_head(_T_IQ + h * IDX_DIM, IDX_DIM, cos64, sin64)
        r = h * IDX_DIM
        iqt_ref[r:r + 32, :] = (o1 * (IDX_DIM ** -0.5)).astype(BF16)
        iqt_ref[r + 32:r + 64, :] = (o2 * (IDX_DIM ** -0.5)).astype(BF16)

    for h in range(R_HEADS):
        o1, o2 = rope_head(_T_RQ + h * R_QK_DIM, R_QK_DIM, cos128, sin128)
        q = jnp.concatenate([o1, o2], axis=0)
        rq_ref[:, h * R_QK_DIM:(h + 1) * R_QK_DIM] = q.T.astype(BF16)
        k1, k2 = rope_head(_T_RK + h * R_QK_DIM, R_QK_DIM, cos128, sin128)
        r = h * R_QK_DIM
        rkt_ref[r:r + 64, :] = (k1 * (R_QK_DIM ** -0.5)).astype(BF16)
        rkt_ref[r + 64:r + 128, :] = (k2 * (R_QK_DIM ** -0.5)).astype(BF16)

    a1, a2 = rope_head(_T_AK, A_HEAD_DIM, cos64, sin64)
    i1, i2 = rope_head(_T_IK, IDX_DIM, cos64, sin64)
    kk = jnp.concatenate([a1, a2, i1, i2], axis=0)
    kk_ref[...] = kk.T.astype(BF16)

    avt_ref[...] = zt[_T_AV:_T_AV + A_HEAD_DIM].astype(BF16)
    iwt_ref[...] = zt[_T_IW:_T_IW + IDX_HEADS] * (IDX_HEADS ** -0.5)


def _inproj(h, g, pos3, wt, wn, tm):
    t = h.shape[0]
    nt = t // tm
    row = lambda i: (i, 0)
    col = lambda i: (0, i)
    outs = [
        (jax.ShapeDtypeStruct((A_WIDTH, t), BF16), pl.BlockSpec((A_WIDTH, tm), col)),
        (jax.ShapeDtypeStruct((A_WIDTH, t), BF16), pl.BlockSpec((A_WIDTH, tm), col)),
        (jax.ShapeDtypeStruct((t, 512), BF16), pl.BlockSpec((tm, 512), row)),
        (jax.ShapeDtypeStruct((512, t), BF16), pl.BlockSpec((512, tm), col)),
        (jax.ShapeDtypeStruct((t, 128), BF16), pl.BlockSpec((tm, 128), row)),
        (jax.ShapeDtypeStruct((A_HEAD_DIM, t), BF16), pl.BlockSpec((A_HEAD_DIM, tm), col)),
        (jax.ShapeDtypeStruct((IDX_HEADS, t), F32), pl.BlockSpec((IDX_HEADS, tm), col)),
        (jax.ShapeDtypeStruct((t, R_WIDTH), BF16), pl.BlockSpec((tm, R_WIDTH), row)),
        (jax.ShapeDtypeStruct((t, R_WIDTH), BF16), pl.BlockSpec((tm, R_WIDTH), row)),
        (jax.ShapeDtypeStruct((t, D_MODEL), BF16), pl.BlockSpec((tm, D_MODEL), row)),
        (jax.ShapeDtypeStruct((t, D_MODEL), BF16), pl.BlockSpec((tm, D_MODEL), row)),
    ]
    return pl.pallas_call(
        _inproj_kernel,
        grid=(nt,),
        in_specs=[
            pl.BlockSpec((tm, D_MODEL), row),
            _const_spec((1, D_MODEL)),
            pl.BlockSpec((1, 1, tm), lambda i: (i, 0, 0)),
            _const_spec((_T_ROWS, D_MODEL)),
            _const_spec((D_MODEL, _N_TOKEN_MAJOR)),
        ],
        out_specs=[o[1] for o in outs],
        out_shape=[o[0] for o in outs],
        compiler_params=pltpu.CompilerParams(
            dimension_semantics=("parallel",), vmem_limit_bytes=VMEM_LIMIT),
        name="inproj",
    )(h, g, pos3, wt, wn)


def _dsa_kernel(kk_ref, avt_ref, aqt_ref, iqt_ref, iwt_ref, o_ref,
                key_ref, qa_ref, qi_ref, acc_ref, thr_ref, *, n_sel):
    j = pl.program_id(1)
    n_tiles = (j + 2) // 2
    nq = Q_BLOCK
    hq = A_HEADS * nq

    zeros = jnp.zeros((A_HEAD_DIM, nq), BF16)
    for h in range(A_HEADS):
        cs = slice(h * nq, (h + 1) * nq)
        qa_ref[0:64, cs] = aqt_ref[h * 64:(h + 1) * 64, :]
        qa_ref[64:128, cs] = zeros
        qi_ref[0:64, cs] = zeros
        qi_ref[64:128, cs] = iqt_ref[h * 64:(h + 1) * 64, :]

    lane = lax.broadcasted_iota(I32, (1, nq), 1)
    vis_end = (j * nq + (lane // CHUNK + 1) * CHUNK)
    w = iwt_ref[...]

    def score_tile(t, carry):
        r0 = pl.multiple_of(t * KEY_TILE, KEY_TILE)
        kt = kk_ref[pl.ds(r0, KEY_TILE), :]
        rel = jnp.maximum(_dot(kt, qi_ref[...]), 0.0)
        sc = rel[:, 0:nq] * w[0:1, :]
        for h in range(1, IDX_HEADS):
            sc = sc + rel[:, h * nq:(h + 1) * nq] * w[h:h + 1, :]
        sc = jnp.where(sc == 0.0, 0.0, sc)
        spos = r0 + lax.broadcasted_iota(I32, (KEY_TILE, nq), 0)
        sc = jnp.where(spos < vis_end, sc, NEG_BIG)
        bits = pltpu.bitcast(sc, I32)
        key_ref[pl.ds(r0, KEY_TILE), :] = bits ^ ((bits >> 31) & 0x7FFFFFFF)
        return carry

    lax.fori_loop(0, n_tiles, score_tile, 0)

    def count_where(pred):
        def body(t, acc):
            r0 = pl.multiple_of(t * KEY_TILE, KEY_TILE)
            k = key_ref[pl.ds(r0, KEY_TILE), :]
            spos = r0 + lax.broadcasted_iota(I32, (KEY_TILE, nq), 0)
            ones = jnp.where(pred(k, spos), 1, 0).astype(I32)
            return acc + jnp.sum(ones.reshape(KEY_TILE // 8, 8, nq), axis=0)
        acc = lax.fori_loop(0, n_tiles, body, jnp.zeros((8, nq), I32))
        return jnp.sum(acc, axis=0, keepdims=True)

    def bit_step(i, thr):
        cand = thr + jnp.left_shift(jnp.int32(1), 31 - i)
        cnt = count_where(lambda k, _: k >= cand)
        return jnp.where(cnt >= n_sel, cand, thr)

    thr = lax.fori_loop(0, 32, bit_step, jnp.full((1, nq), INT_MIN, I32))

    cnt_gt = count_where(lambda k, _: k > thr)
    cnt_ge = count_where(lambda k, _: k >= thr)
    live = thr > KEY_INVALID
    thr_ref[0:1, :] = jnp.where(live, jnp.int32(2 ** 31 - 1), jnp.int32(-1))
    need_ties = jnp.max(jnp.where(live & (cnt_ge > n_sel), 1, 0)) > 0

    @pl.when(need_ties)
    def _():
        want = n_sel - cnt_gt

        def idx_step(i, m):
            cand = m + jnp.left_shift(jnp.int32(1), 10 - i)
            cnt = count_where(lambda k, spos: (k == thr) & (spos < cand))
            return jnp.where(cnt < want, cand, m)

        m = lax.fori_loop(0, 11, idx_step, jnp.zeros((1, nq), I32))
        thr_ref[0:1, :] = jnp.where(live, m, jnp.int32(-1))

    tie_max = thr_ref[0:1, :]
    thr_gt = jnp.maximum(thr, KEY_INVALID)

    acc_ref[...] = jnp.zeros(acc_ref.shape, F32)

    def attn_tile(t, carry):
        m_run, l_run = carry
        r0 = pl.multiple_of(t * KEY_TILE, KEY_TILE)
        kt = kk_ref[pl.ds(r0, KEY_TILE), :]
        k = key_ref[pl.ds(r0, KEY_TILE), :]
        spos = r0 + lax.broadcasted_iota(I32, (KEY_TILE, nq), 0)
        sel = (k > thr_gt) | ((k == thr) & (spos <= tie_max))
        lg = _dot(kt, qa_ref[...])
        lg = jnp.concatenate(
            [jnp.where(sel, lg[:, h * nq:(h + 1) * nq], 2.0 * NEG_BIG) for h in range(A_HEADS)], axis=1)
        m_new = jnp.maximum(m_run, jnp.max(lg, axis=0, keepdims=True))
        alpha = jnp.exp(m_run - m_new)
        p = jnp.exp(lg - m_new)
        l_new = alpha * l_run + jnp.sum(p, axis=0, keepdims=True)
        vt = avt_ref[:, pl.ds(r0, KEY_TILE)]
        acc_ref[...] = acc_ref[...] * alpha + _dot(vt, p.astype(BF16))
        return m_new, l_new

    m0 = jnp.full((1, hq), NEG_BIG, F32)
    l0 = jnp.zeros((1, hq), F32)
    _, l_fin = lax.fori_loop(0, n_tiles, attn_tile, (m0, l0))

    out_t = acc_ref[...] / l_fin
    for hp in range(A_HEADS // 2):
        blk = jnp.concatenate(
            [out_t[:, (2 * hp) * nq:(2 * hp + 1) * nq], out_t[:, (2 * hp + 1) * nq:(2 * hp + 2) * nq]], axis=0)
        o_ref[:, hp * 128:(hp + 1) * 128] = blk.T.astype(o_ref.dtype)


def _dsa(kk, avt, aqt, iqt, iwt, batch, seq, n_sel):
    t = batch * seq
    nqb = seq // Q_BLOCK
    return pl.pallas_call(
        functools.partial(_dsa_kernel, n_sel=n_sel),
        grid=(batch, nqb),
        in_specs=[
            pl.BlockSpec((seq, 128), lambda b, j: (b, 0)),
            pl.BlockSpec((A_HEAD_DIM, seq), lambda b, j: (0, b)),
            pl.BlockSpec((A_WIDTH, Q_BLOCK), lambda b, j: (0, b * nqb + j)),
            pl.BlockSpec((A_WIDTH, Q_BLOCK), lambda b, j: (0, b * nqb + j)),
            pl.BlockSpec((IDX_HEADS, Q_BLOCK), lambda b, j: (0, b * nqb + j)),
        ],
        out_specs=pl.BlockSpec((Q_BLOCK, A_WIDTH), lambda b, j: (b * nqb + j, 0)),
        out_shape=jax.ShapeDtypeStruct((t, A_WIDTH), BF16),
        scratch_shapes=[
            pltpu.VMEM((seq, Q_BLOCK), I32),
            pltpu.VMEM((128, A_HEADS * Q_BLOCK), BF16),
            pltpu.VMEM((128, A_HEADS * Q_BLOCK), BF16),
            pltpu.VMEM((A_HEAD_DIM, A_HEADS * Q_BLOCK), F32),
            pltpu.VMEM((8, Q_BLOCK), I32),
        ],
        compiler_params=pltpu.CompilerParams(
            dimension_semantics=("parallel", "parallel"), vmem_limit_bytes=VMEM_LIMIT),
        name="dsa",
    )(kk, avt, aqt, iqt, iwt)


def _log_gamma(h):
    return math.log1p(-(2.0 ** (-5.0 - h)))


def _retention_kernel(rq_ref, rkt_ref, rv_ref, rg_ref, gn_ref, o_ref,
                      state_ref, dmat_ref, dq_ref):
    nb = RET_BLOCK
    first = (pl.program_id(0) == 0) & (pl.program_id(1) == 0)

    @pl.when(first)
    def _():
        i = lax.broadcasted_iota(I32, (nb, nb), 0)
        jj = lax.broadcasted_iota(I32, (nb, nb), 1)
        dist = jnp.abs(i - jj).astype(F32)
        seen = (jj // CHUNK) <= (i // CHUNK)
        for h in range(R_HEADS):
            lg = _log_gamma(h)
            dmat_ref[h] = jnp.where(seen, jnp.exp(lg * dist), 0.0)
            dq_ref[h] = jnp.exp(lg * (i.astype(F32) + 1.0))

    @pl.when(pl.program_id(1) == 0)
    def _():
        state_ref[...] = jnp.zeros(state_ref.shape, F32)

    jrow = lax.broadcasted_iota(I32, (1, nb), 1).astype(F32)
    for h in range(R_HEADS):
        lg = _log_gamma(h)
        q = rq_ref[:, h * R_QK_DIM:(h + 1) * R_QK_DIM]
        kt = rkt_ref[h * R_QK_DIM:(h + 1) * R_QK_DIM, :]
        v = rv_ref[:, h * R_V_DIM:(h + 1) * R_V_DIM]
        state = state_ref[h]
        s = _dot(q, kt) * dmat_ref[h]
        y = _dot(s.astype(BF16), v) + _dot(q, state.astype(BF16)) * dq_ref[h]
        dk = jnp.exp(lg * (nb - 1.0 - jrow))
        kd = (kt.astype(F32) * dk).astype(BF16)
        state_ref[h] = state * math.exp(lg * nb) + _dot(kd, v)
        mean = jnp.mean(y, axis=-1, keepdims=True)
        yc = y - mean
        var = jnp.mean(yc * yc, axis=-1, keepdims=True)
        yn = yc * lax.rsqrt(var + GN_EPS) * gn_ref[:, h * R_V_DIM:(h + 1) * R_V_DIM]
        gate = rg_ref[:, h * R_V_DIM:(h + 1) * R_V_DIM].astype(F32)
        o_ref[:, h * R_V_DIM:(h + 1) * R_V_DIM] = (yn * (gate * jax.nn.sigmoid(gate))).astype(o_ref.dtype)


def _retention(rq, rkt, rv, rg, gn, batch, seq):
    t = batch * seq
    nblk = seq // RET_BLOCK
    tok = lambda b, n: (b * nblk + n, 0)
    return pl.pallas_call(
        _retention_kernel,
        grid=(batch, nblk),
        in_specs=[
            pl.BlockSpec((RET_BLOCK, 512), tok),
            pl.BlockSpec((512, RET_BLOCK), lambda b, n: (0, b * nblk + n)),
            pl.BlockSpec((RET_BLOCK, R_WIDTH), tok),
            pl.BlockSpec((RET_BLOCK, R_WIDTH), tok),
            pl.BlockSpec((1, R_WIDTH), lambda b, n: (0, 0)),
        ],
        out_specs=pl.BlockSpec((RET_BLOCK, R_WIDTH), tok),
        out_shape=jax.ShapeDtypeStruct((t, R_WIDTH), BF16),
        scratch_shapes=[
            pltpu.VMEM((R_HEADS, R_QK_DIM, R_V_DIM), F32),
            pltpu.VMEM((R_HEADS, RET_BLOCK, RET_BLOCK), F32),
            pltpu.VMEM((R_HEADS, RET_BLOCK, RET_BLOCK), F32),
        ],
        compiler_params=pltpu.CompilerParams(
            dimension_semantics=("arbitrary", "arbitrary"), vmem_limit_bytes=VMEM_LIMIT),
        name="retention",
    )(rq, rkt, rv, rg, gn)


def _merge_kernel(h_ref, ya_ref, yr_ref, ga_ref, gb_ref, wa_ref, wb_ref, wo_ref, o_ref):
    a = _dot(ya_ref[...], wa_ref[...])
    b = _dot(yr_ref[...], wb_ref[...])
    merged = (jax.nn.sigmoid(ga_ref[...].astype(F32)) * a
              + jax.nn.sigmoid(gb_ref[...].astype(F32)) * b)
    o_ref[...] = h_ref[...] + _dot(merged.astype(BF16), wo_ref[...])


def _merge(h, ya, yr, ga, gb, wa, wb, wo, tm):
    t = h.shape[0]
    row = lambda i: (i, 0)
    return pl.pallas_call(
        _merge_kernel,
        grid=(t // tm,),
        in_specs=[
            pl.BlockSpec((tm, D_MODEL), row),
            pl.BlockSpec((tm, A_WIDTH), row),
            pl.BlockSpec((tm, R_WIDTH), row),
            pl.BlockSpec((tm, D_MODEL), row),
            pl.BlockSpec((tm, D_MODEL), row),
            _const_spec((A_WIDTH, D_MODEL)),
            _const_spec((R_WIDTH, D_MODEL)),
            _const_spec((D_MODEL, D_MODEL)),
        ],
        out_specs=pl.BlockSpec((tm, D_MODEL), row),
        out_shape=jax.ShapeDtypeStruct((t, D_MODEL), F32),
        compiler_params=pltpu.CompilerParams(
            dimension_semantics=("parallel",), vmem_limit_bytes=VMEM_LIMIT),
        name="merge",
    )(h, ya, yr, ga, gb, wa, wb, wo)


def _ple_kernel(h_ref, p_ref, g_ref, wg_ref, wp_ref, gf_ref, o_ref):
    h = h_ref[...]
    u = _rms(h, g_ref[...]).astype(BF16)
    gate = jax.nn.sigmoid(_dot(u, wg_ref[...]))
    emb = _dot(p_ref[...].astype(BF16), wp_ref[...])
    o_ref[...] = _rms(h + gate * emb, gf_ref[...])


def _ple(h, p, g, wg, wp, gf, tm):
    t = h.shape[0]
    row = lambda i: (i, 0)
    return pl.pallas_call(
        _ple_kernel,
        grid=(t // tm,),
        in_specs=[
            pl.BlockSpec((tm, D_MODEL), row),
            pl.BlockSpec((tm, PLE_DIM), row),
            _const_spec((1, D_MODEL)),
            _const_spec((D_MODEL, D_MODEL)),
            _const_spec((PLE_DIM, D_MODEL)),
            _const_spec((1, D_MODEL)),
        ],
        out_specs=pl.BlockSpec((tm, D_MODEL), row),
        out_shape=jax.ShapeDtypeStruct((t, D_MODEL), F32),
        compiler_params=pltpu.CompilerParams(
            dimension_semantics=("parallel",), vmem_limit_bytes=VMEM_LIMIT),
        name="ple",
    )(h, p, g, wg, wp, gf)


def _token_tile(t):
    return 512 if t % 512 == 0 else 256


def kernel(x, p, positions, ffn1_norm, ffn1_w_gate, ffn1_w_up, ffn1_w_down, mix_norm, w_in, ret_gn, w_branch_a, w_branch_b, w_out, ffn2_norm, ffn2_w_gate, ffn2_w_up, ffn2_w_down, ple_norm, w_ple_gate, w_ple_proj, final_norm):
    batch, seq, _ = x.shape
    depth = p.shape[0]
    t = batch * seq
    tm = _token_tile(t)
    n_sel = min(TOPK_MAX, seq // 4)
    assert depth == 1, "the final norm is fused into the per-layer embedding step"
    assert seq % RET_BLOCK == 0 and seq % KEY_TILE == 0 and n_sel <= seq

    h = x.reshape(t, D_MODEL)
    pos3 = positions.reshape(t // tm, 1, tm)
    vec = lambda g: g.reshape(1, -1).astype(F32)

    for i in range(depth):
        w = w_in[i]
        wt = jnp.concatenate([
            w[:, _OFF_AQ:_OFF_AQ + A_WIDTH], w[:, _OFF_IQ:_OFF_IQ + IDX_HEADS * IDX_DIM],
            w[:, _OFF_RQ:_OFF_RQ + 512], w[:, _OFF_RK:_OFF_RK + 512],
            w[:, _OFF_AK:_OFF_AK + 64], w[:, _OFF_IK:_OFF_IK + 64],
            w[:, _OFF_AV:_OFF_AV + 64], w[:, _OFF_IW:_OFF_IW + IDX_HEADS]], axis=1).T.astype(BF16)
        wn = w[:, _OFF_RV:].astype(BF16)

        h = _ffn(h, vec(ffn1_norm[i]), ffn1_w_gate[i].astype(BF16), ffn1_w_up[i].astype(BF16),
                 ffn1_w_down[i].astype(BF16), tm)
        aqt, iqt, rq, rkt, kk, avt, iwt, rv, rg, ga, gb = _inproj(h, vec(mix_norm[i]), pos3, wt, wn, tm)
        ya = _dsa(kk, avt, aqt, iqt, iwt, batch, seq, n_sel)
        yr = _retention(rq, rkt, rv, rg, vec(ret_gn[i]), batch, seq)
        h = _merge(h, ya, yr, ga, gb, w_branch_a[i].astype(BF16), w_branch_b[i].astype(BF16),
                   w_out[i].astype(BF16), tm)
        h = _ffn(h, vec(ffn2_norm[i]), ffn2_w_gate[i].astype(BF16), ffn2_w_up[i].astype(BF16),
                 ffn2_w_down[i].astype(BF16), tm)
        h = _ple(h, p[i].reshape(t, PLE_DIM), vec(ple_norm[i]), w_ple_gate[i].astype(BF16),
                 w_ple_proj[i].astype(BF16), vec(final_norm), tm)
    return h.reshape(batch, seq, D_MODEL)
```

```python
import functools
import math

import jax
import jax.numpy as jnp
import numpy as np
from jax import lax
from jax.experimental import pallas as pl
from jax.experimental.pallas import tpu as pltpu

F32 = jnp.float32
BF16 = jnp.bfloat16
I32 = jnp.int32

D_MODEL = 1024
CHUNK = 64
ROPE_THETA = 10000.0
EPS = 1e-6
GN_EPS = 1e-5
NEG_BIG = -1e30
A_HEADS = 8
A_HEAD_DIM = 64
IDX_HEADS = 8
IDX_DIM = 64
TOPK_MAX = 256
A_WIDTH = A_HEADS * A_HEAD_DIM
R_HEADS = 4
R_QK_DIM = 128
R_V_DIM = 256
R_WIDTH = R_HEADS * R_V_DIM
D_FF = 2816
PLE_DIM = 256

_OFF_AQ = 0
_OFF_AK = _OFF_AQ + A_WIDTH
_OFF_AV = _OFF_AK + A_HEAD_DIM
_OFF_IQ = _OFF_AV + A_HEAD_DIM
_OFF_IK = _OFF_IQ + IDX_HEADS * IDX_DIM
_OFF_IW = _OFF_IK + IDX_DIM
_OFF_RQ = _OFF_IW + IDX_HEADS
_OFF_RK = _OFF_RQ + R_HEADS * R_QK_DIM
_OFF_RV = _OFF_RK + R_HEADS * R_QK_DIM
_N_TOKEN_MAJOR = 2 * R_WIDTH + 2 * D_MODEL

_T_AQ = 0
_T_IQ = _T_AQ + A_WIDTH
_T_RQ = _T_IQ + IDX_HEADS * IDX_DIM
_T_RK = _T_RQ + R_HEADS * R_QK_DIM
_T_AK = _T_RK + R_HEADS * R_QK_DIM
_T_IK = _T_AK + A_HEAD_DIM
_T_AV = _T_IK + IDX_DIM
_T_IW = _T_AV + A_HEAD_DIM
_T_ROWS = _T_IW + IDX_HEADS

VMEM_LIMIT = 56 * 1024 * 1024
FF_CHUNK = 256
DSA_QB = 256
KEY_TILE = 256
KEY_HALF = KEY_TILE // 2
RET_BLOCK = 256

INT_MIN = -(2 ** 31)
WORD = 32
LOG2_E = math.log2(math.e)


def _const_spec(shape):
    nd = len(shape)
    return pl.BlockSpec(shape, lambda *_: (0,) * nd, pipeline_mode=pl.Buffered(1))


def _rms(x, g):
    ms = jnp.mean(x * x, axis=-1, keepdims=True)
    return x * lax.rsqrt(ms + EPS) * g


def _dot(a, b):
    return jnp.dot(a, b, preferred_element_type=F32)


def _ffn_kernel(h_ref, g_ref, wg_ref, wu_ref, wd_ref, o_ref):
    h = h_ref[...]
    u = _rms(h, g_ref[...]).astype(BF16)
    acc = jnp.zeros(h.shape, F32)
    for c in range(D_FF // FF_CHUNK):
        sl = slice(c * FF_CHUNK, (c + 1) * FF_CHUNK)
        gate = _dot(u, wg_ref[:, sl])
        up = _dot(u, wu_ref[:, sl])
        act = (gate * jax.nn.sigmoid(gate) * up).astype(BF16)
        acc = acc + _dot(act, wd_ref[sl, :])
    o_ref[...] = h + 0.5 * acc


def _ffn(h, g, wg, wu, wd, tm):
    t = h.shape[0]
    return pl.pallas_call(
        _ffn_kernel,
        grid=(t // tm,),
        in_specs=[
            pl.BlockSpec((tm, D_MODEL), lambda i: (i, 0)),
            _const_spec((1, D_MODEL)),
            _const_spec((D_MODEL, D_FF)),
            _const_spec((D_MODEL, D_FF)),
            _const_spec((D_FF, D_MODEL)),
        ],
        out_specs=pl.BlockSpec((tm, D_MODEL), lambda i: (i, 0)),
        out_shape=jax.ShapeDtypeStruct((t, D_MODEL), F32),
        compiler_params=pltpu.CompilerParams(
            dimension_semantics=("parallel",), vmem_limit_bytes=VMEM_LIMIT),
        name="ffn",
    )(h, g, wg, wu, wd)


def _rope_rows(t1, t2, cos, sin):
    return t1 * cos - t2 * sin, t2 * cos + t1 * sin


def _inproj_kernel(h_ref, g_ref, pos_ref, wt_ref, wn_ref,
                   aqt_ref, iqt_ref, rq_ref, rkt_ref, kk_ref, avt_ref, iwt_ref,
                   rv_ref, rg_ref, ga_ref, gb_ref):
    tm = h_ref.shape[0]
    u = _rms(h_ref[...], g_ref[...]).astype(BF16)

    zn = _dot(u, wn_ref[...])
    rv_ref[...] = zn[:, 0:R_WIDTH].astype(BF16)
    rg_ref[...] = zn[:, R_WIDTH:2 * R_WIDTH].astype(BF16)
    ga_ref[...] = zn[:, 2 * R_WIDTH:2 * R_WIDTH + D_MODEL].astype(BF16)
    gb_ref[...] = zn[:, 2 * R_WIDTH + D_MODEL:].astype(BF16)

    zt = lax.dot_general(wt_ref[...], u, (((1,), (1,)), ((), ())),
                         preferred_element_type=F32)

    pos = pos_ref[0].astype(F32)

    def angles(half):
        i = lax.broadcasted_iota(I32, (half, tm), 0).astype(F32)
        inv = jnp.exp(i * (-math.log(ROPE_THETA) / half))
        ang = pos * inv
        return jnp.cos(ang), jnp.sin(ang)

    cos64, sin64 = angles(A_HEAD_DIM // 2)
    cos128, sin128 = angles(R_QK_DIM // 2)

    def rope_head(row0, dim, cos, sin):
        half = dim // 2
        return _rope_rows(zt[row0:row0 + half], zt[row0 + half:row0 + dim], cos, sin)

    for h in range(A_HEADS):
        o1, o2 = rope_head(_T_AQ + h * A_HEAD_DIM, A_HEAD_DIM, cos64, sin64)
        r = h * A_HEAD_DIM
        aqt_ref[r:r + 32, :] = (o1 * (A_HEAD_DIM ** -0.5 * LOG2_E)).astype(BF16)
        aqt_ref[r + 32:r + 64, :] = (o2 * (A_HEAD_DIM ** -0.5 * LOG2_E)).astype(BF16)
    for h in range(IDX_HEADS):
        o1, o2 = rope_head(_T_IQ + h * IDX_DIM, IDX_DIM, cos64, sin64)
        r = h * IDX_DIM
        iqt_ref[r:r + 32, :] = (o1 * (IDX_DIM ** -0.5)).astype(BF16)
        iqt_ref[r + 32:r + 64, :] = (o2 * (IDX_DIM ** -0.5)).astype(BF16)

    for h in range(R_HEADS):
        o1, o2 = rope_head(_T_RQ + h * R_QK_DIM, R_QK_DIM, cos128, sin128)
        q = jnp.concatenate([o1, o2], axis=0)
        rq_ref[:, h * R_QK_DIM:(h + 1) * R_QK_DIM] = q.T.astype(BF16)
        k1, k2 = rope_head(_T_RK + h * R_QK_DIM, R_QK_DIM, cos128, sin128)
        r = h * R_QK_DIM
        rkt_ref[r:r + 64, :] = (k1 * (R_QK_DIM ** -0.5)).astype(BF16)
        rkt_ref[r + 64:r + 128, :] = (k2 * (R_QK_DIM ** -0.5)).astype(BF16)

    a1, a2 = rope_head(_T_AK, A_HEAD_DIM, cos64, sin64)
    i1, i2 = rope_head(_T_IK, IDX_DIM, cos64, sin64)
    kk = jnp.concatenate([a1, a2, i1, i2], axis=0)
    kk_ref[...] = kk.T.astype(BF16)

    avt_ref[...] = zt[_T_AV:_T_AV + A_HEAD_DIM].astype(BF16)
    iwt_ref[...] = zt[_T_IW:_T_IW + IDX_HEADS] * (IDX_HEADS ** -0.5)


def _inproj(h, g, pos3, wt, wn, tm):
    t = h.shape[0]
    nt = t // tm
    row = lambda i: (i, 0)
    col = lambda i: (0, i)
    outs = [
        (jax.ShapeDtypeStruct((A_WIDTH, t), BF16), pl.BlockSpec((A_WIDTH, tm), col)),
        (jax.ShapeDtypeStruct((A_WIDTH, t), BF16), pl.BlockSpec((A_WIDTH, tm), col)),
        (jax.ShapeDtypeStruct((t, 512), BF16), pl.BlockSpec((tm, 512), row)),
        (jax.ShapeDtypeStruct((512, t), BF16), pl.BlockSpec((512, tm), col)),
        (jax.ShapeDtypeStruct((t, 128), BF16), pl.BlockSpec((tm, 128), row)),
        (jax.ShapeDtypeStruct((A_HEAD_DIM, t), BF16), pl.BlockSpec((A_HEAD_DIM, tm), col)),
        (jax.ShapeDtypeStruct((IDX_HEADS, t), F32), pl.BlockSpec((IDX_HEADS, tm), col)),
        (jax.ShapeDtypeStruct((t, R_WIDTH), BF16), pl.BlockSpec((tm, R_WIDTH), row)),
        (jax.ShapeDtypeStruct((t, R_WIDTH), BF16), pl.BlockSpec((tm, R_WIDTH), row)),
        (jax.ShapeDtypeStruct((t, D_MODEL), BF16), pl.BlockSpec((tm, D_MODEL), row)),
        (jax.ShapeDtypeStruct((t, D_MODEL), BF16), pl.BlockSpec((tm, D_MODEL), row)),
    ]
    return pl.pallas_call(
        _inproj_kernel,
        grid=(nt,),
        in_specs=[
            pl.BlockSpec((tm, D_MODEL), row),
            _const_spec((1, D_MODEL)),
            pl.BlockSpec((1, 1, tm), lambda i: (i, 0, 0)),
            _const_spec((_T_ROWS, D_MODEL)),
            _const_spec((D_MODEL, _N_TOKEN_MAJOR)),
        ],
        out_specs=[o[1] for o in outs],
        out_shape=[o[0] for o in outs],
        compiler_params=pltpu.CompilerParams(
            dimension_semantics=("parallel",), vmem_limit_bytes=VMEM_LIMIT),
        name="inproj",
    )(h, g, pos3, wt, wn)


def _tree_or(parts):
    while len(parts) > 1:
        parts = [parts[i] | parts[i + 1] for i in range(0, len(parts) - 1, 2)] + (
            [parts[-1]] if len(parts) % 2 else [])
    return parts[0]


_BT_MASKS = {16: 0x0000FFFF, 8: 0x00FF00FF, 4: 0x0F0F0F0F, 2: 0x33333333, 1: 0x55555555}


def _bit_transpose32(rows):
    a = list(rows)
    j = 16
    while j:
        mask = _BT_MASKS[j]
        k = 0
        while k < WORD:
            t = (a[k] ^ (a[k + j] >> j)) & mask
            a[k] = a[k] ^ t
            a[k + j] = a[k + j] ^ (t << j)
            k = (k + j + 1) & ~j
        j >>= 1
    return a


def _word_bit(m):
    v = 1 << (WORD - 1 - m)
    return v - (1 << WORD) if v >= (1 << (WORD - 1)) else v


def _dsa_kernel(kk_ref, avt_ref, aqt_ref, iqt_ref, iwt_ref, o_ref,
                lg_ref, planes_ref, posp_ref, valid_ref, sel_ref, qa_ref, qi_ref, acc_ref, ml_ref,
                *, n_sel, seq):
    j = pl.program_id(1)
    nq = DSA_QB
    n_tiles = j + 1
    words = seq // WORD
    wpt = KEY_TILE // WORD
    pos_bits = (seq - 1).bit_length()

    @pl.when((pl.program_id(0) == 0) & (j == 0))
    def _():
        qa_ref[64:128, :] = jnp.zeros((64, A_HEADS * nq), BF16)
        qi_ref[0:64, :] = jnp.zeros((64, A_HEADS * nq), BF16)
        planes_ref[...] = jnp.zeros(planes_ref.shape, I32)
        valid_ref[...] = jnp.zeros(valid_ref.shape, I32)
        for t in range(seq // KEY_TILE):
            sub = lax.broadcasted_iota(I32, (wpt, nq), 0)
            p = _bit_transpose32([(seq - 1 - t * KEY_TILE - m * wpt) - sub for m in range(WORD)])
            for i in range(pos_bits):
                posp_ref[i, t * wpt:(t + 1) * wpt, :] = p[WORD - pos_bits + i]

    for h in range(A_HEADS):
        cs = slice(h * nq, (h + 1) * nq)
        qa_ref[0:64, cs] = aqt_ref[h * 64:(h + 1) * 64, :]
        qi_ref[64:128, cs] = iqt_ref[h * 64:(h + 1) * 64, :]

    lane = lax.broadcasted_iota(I32, (1, nq), 1)
    vis_end = (j * nq + (lane // CHUNK + 1) * CHUNK)
    w = iwt_ref[...]

    def tile_rows(t):
        return pl.ds(pl.multiple_of(t * KEY_TILE, KEY_TILE), KEY_TILE)

    def word_rows(t):
        return pl.ds(pl.multiple_of(t * wpt, wpt), wpt)

    def score_tile(t, carry):
        kt_all = kk_ref[tile_rows(t), :]
        for h in range(A_HEADS):
            cs = slice(h * nq, (h + 1) * nq)
            lg_ref[tile_rows(t), cs] = _dot(kt_all, qa_ref[:, cs])

        keys, valid = [], []
        for half in range(2):
            r0 = pl.multiple_of(t * KEY_TILE + half * KEY_HALF, KEY_HALF)
            kt = kk_ref[pl.ds(r0, KEY_HALF), :]
            sc = None
            for h in range(IDX_HEADS):
                rel = jnp.maximum(_dot(kt, qi_ref[:, h * nq:(h + 1) * nq]), 0.0)
                part = rel * w[h:h + 1, :]
                sc = part if sc is None else sc + part
            sc = jnp.where(sc == 0.0, 0.0, sc)
            spos = r0 + lax.broadcasted_iota(I32, (KEY_HALF, nq), 0)
            sc = jnp.where(spos < vis_end, sc, NEG_BIG)
            bits = pltpu.bitcast(sc, I32)
            keys.append(bits ^ ((bits >> 31) | jnp.int32(INT_MIN)))
            valid.append(sc > 0.5 * NEG_BIG)

        def member(parts, m):
            r = (m % 16) * 8
            return parts[m // 16][r:r + 8, :]

        planes = _bit_transpose32([member(keys, m) for m in range(WORD)])
        for i in range(WORD):
            planes_ref[i, word_rows(t), :] = planes[i]
        valid_ref[word_rows(t), :] = _tree_or(
            [jnp.where(member(valid, m), jnp.int32(_word_bit(m)), jnp.int32(0)) for m in range(WORD)])
        return carry

    lax.fori_loop(0, n_tiles, score_tile, 0)

    wrow = lax.broadcasted_iota(I32, (words, nq), 0)
    state = (jnp.where(wrow < n_tiles * wpt, jnp.int32(-1), jnp.int32(0)),
             jnp.zeros((words, nq), I32),
             jnp.zeros((1, nq), I32))

    def radix_step(plane, state):
        alive, sel, above = state
        ones = alive & plane
        c = jnp.sum(lax.population_count(ones), axis=0, keepdims=True)
        keep_ones = (above + c) >= n_sel
        return (jnp.where(keep_ones, ones, alive ^ ones),
                jnp.where(keep_ones, sel, sel | ones),
                jnp.where(keep_ones, above, above + c))

    state = lax.fori_loop(0, WORD, lambda i, s: radix_step(planes_ref[i], s), state)
    state = lax.fori_loop(0, pos_bits, lambda i, s: radix_step(posp_ref[i], s), state)
    alive, sel, _ = state
    sel_ref[...] = (sel | alive) & valid_ref[...]

    acc_ref[...] = jnp.zeros(acc_ref.shape, F32)
    ml_ref[0:1, :] = jnp.full((1, A_HEADS * nq), 0.5 * NEG_BIG, F32)
    ml_ref[1:2, :] = jnp.zeros((1, A_HEADS * nq), F32)

    def attn_tile(t, carry):
        rows = tile_rows(t)
        vt = avt_ref[:, rows]
        selw = sel_ref[word_rows(t), :]
        bias = jnp.concatenate(
            [jnp.where((selw << m) < 0, 0.0, NEG_BIG) for m in range(WORD)], axis=0)
        for h in range(A_HEADS):
            cs = slice(h * nq, (h + 1) * nq)
            lg = lg_ref[rows, cs] + bias
            m_old = ml_ref[0:1, cs]
            m_new = jnp.maximum(m_old, jnp.max(lg, axis=0, keepdims=True))
            alpha = jnp.exp2(m_old - m_new)
            p = jnp.exp2(lg - m_new)
            ml_ref[0:1, cs] = m_new
            ml_ref[1:2, cs] = alpha * ml_ref[1:2, cs] + jnp.sum(p, axis=0, keepdims=True)
            acc_ref[:, cs] = acc_ref[:, cs] * alpha + _dot(vt, p.astype(BF16))
        return carry

    lax.fori_loop(0, n_tiles, attn_tile, 0)

    out_t = acc_ref[...] / ml_ref[1:2, :]
    for hp in range(A_HEADS // 2):
        blk = jnp.concatenate(
            [out_t[:, (2 * hp) * nq:(2 * hp + 1) * nq], out_t[:, (2 * hp + 1) * nq:(2 * hp + 2) * nq]], axis=0)
        o_ref[:, hp * 128:(hp + 1) * 128] = blk.T.astype(o_ref.dtype)


def _dsa(kk, avt, aqt, iqt, iwt, batch, seq, n_sel):
    t = batch * seq
    nqb = seq // DSA_QB
    qblk = lambda b, j: (0, b * nqb + j)
    return pl.pallas_call(
        functools.partial(_dsa_kernel, n_sel=n_sel, seq=seq),
        grid=(batch, nqb),
        in_specs=[
            pl.BlockSpec((seq, 128), lambda b, j: (b, 0)),
            pl.BlockSpec((A_HEAD_DIM, seq), lambda b, j: (0, b)),
            pl.BlockSpec((A_WIDTH, DSA_QB), qblk),
            pl.BlockSpec((A_WIDTH, DSA_QB), qblk),
            pl.BlockSpec((IDX_HEADS, DSA_QB), qblk),
        ],
        out_specs=pl.BlockSpec((DSA_QB, A_WIDTH), lambda b, j: (b * nqb + j, 0)),
        out_shape=jax.ShapeDtypeStruct((t, A_WIDTH), BF16),
        scratch_shapes=[
            pltpu.VMEM((seq, A_HEADS * DSA_QB), F32),
            pltpu.VMEM((WORD, seq // WORD, DSA_QB), I32),
            pltpu.VMEM(((seq - 1).bit_length(), seq // WORD, DSA_QB), I32),
            pltpu.VMEM((seq // WORD, DSA_QB), I32),
            pltpu.VMEM((seq // WORD, DSA_QB), I32),
            pltpu.VMEM((128, A_HEADS * DSA_QB), BF16),
            pltpu.VMEM((128, A_HEADS * DSA_QB), BF16),
            pltpu.VMEM((A_HEAD_DIM, A_HEADS * DSA_QB), F32),
            pltpu.VMEM((8, A_HEADS * DSA_QB), F32),
        ],
        compiler_params=pltpu.CompilerParams(
            dimension_semantics=("arbitrary", "arbitrary"), vmem_limit_bytes=VMEM_LIMIT),
        name="dsa",
    )(kk, avt, aqt, iqt, iwt)


def _log_gamma(h):
    return math.log1p(-(2.0 ** (-5.0 - h)))


def _retention_kernel(rq_ref, rkt_ref, rv_ref, rg_ref, gn_ref, o_ref,
                      state_ref, dmat_ref, dq_ref):
    nb = RET_BLOCK
    first = (pl.program_id(0) == 0) & (pl.program_id(1) == 0)

    @pl.when(first)
    def _():
        i = lax.broadcasted_iota(I32, (nb, nb), 0)
        jj = lax.broadcasted_iota(I32, (nb, nb), 1)
        dist = jnp.abs(i - jj).astype(F32)
        seen = (jj // CHUNK) <= (i // CHUNK)
        for h in range(R_HEADS):
            lg = _log_gamma(h)
            dmat_ref[h] = jnp.where(seen, jnp.exp(lg * dist), 0.0)
            dq_ref[h] = jnp.exp(lg * (i.astype(F32) + 1.0))

    @pl.when(pl.program_id(1) == 0)
    def _():
        state_ref[...] = jnp.zeros(state_ref.shape, F32)

    jrow = lax.broadcasted_iota(I32, (1, nb), 1).astype(F32)
    for h in range(R_HEADS):
        lg = _log_gamma(h)
        q = rq_ref[:, h * R_QK_DIM:(h + 1) * R_QK_DIM]
        kt = rkt_ref[h * R_QK_DIM:(h + 1) * R_QK_DIM, :]
        v = rv_ref[:, h * R_V_DIM:(h + 1) * R_V_DIM]
        state = state_ref[h]
        s = _dot(q, kt) * dmat_ref[h]
        y = _dot(s.astype(BF16), v) + _dot(q, state.astype(BF16)) * dq_ref[h]
        dk = jnp.exp(lg * (nb - 1.0 - jrow))
        kd = (kt.astype(F32) * dk).astype(BF16)
        state_ref[h] = state * math.exp(lg * nb) + _dot(kd, v)
        mean = jnp.mean(y, axis=-1, keepdims=True)
        yc = y - mean
        var = jnp.mean(yc * yc, axis=-1, keepdims=True)
        yn = yc * lax.rsqrt(var + GN_EPS) * gn_ref[:, h * R_V_DIM:(h + 1) * R_V_DIM]
        gate = rg_ref[:, h * R_V_DIM:(h + 1) * R_V_DIM].astype(F32)
        o_ref[:, h * R_V_DIM:(h + 1) * R_V_DIM] = (yn * (gate * jax.nn.sigmoid(gate))).astype(o_ref.dtype)


def _retention(rq, rkt, rv, rg, gn, batch, seq):
    t = batch * seq
    nblk = seq // RET_BLOCK
    tok = lambda b, n: (b * nblk + n, 0)
    return pl.pallas_call(
        _retention_kernel,
        grid=(batch, nblk),
        in_specs=[
            pl.BlockSpec((RET_BLOCK, 512), tok),
            pl.BlockSpec((512, RET_BLOCK), lambda b, n: (0, b * nblk + n)),
            pl.BlockSpec((RET_BLOCK, R_WIDTH), tok),
            pl.BlockSpec((RET_BLOCK, R_WIDTH), tok),
            pl.BlockSpec((1, R_WIDTH), lambda b, n: (0, 0)),
        ],
        out_specs=pl.BlockSpec((RET_BLOCK, R_WIDTH), tok),
        out_shape=jax.ShapeDtypeStruct((t, R_WIDTH), BF16),
        scratch_shapes=[
            pltpu.VMEM((R_HEADS, R_QK_DIM, R_V_DIM), F32),
            pltpu.VMEM((R_HEADS, RET_BLOCK, RET_BLOCK), F32),
            pltpu.VMEM((R_HEADS, RET_BLOCK, RET_BLOCK), F32),
        ],
        compiler_params=pltpu.CompilerParams(
            dimension_semantics=("arbitrary", "arbitrary"), vmem_limit_bytes=VMEM_LIMIT),
        name="retention",
    )(rq, rkt, rv, rg, gn)


def _merge_kernel(h_ref, ya_ref, yr_ref, ga_ref, gb_ref, wa_ref, wb_ref, wo_ref, o_ref):
    a = _dot(ya_ref[...], wa_ref[...])
    b = _dot(yr_ref[...], wb_ref[...])
    merged = (jax.nn.sigmoid(ga_ref[...].astype(F32)) * a
              + jax.nn.sigmoid(gb_ref[...].astype(F32)) * b)
    o_ref[...] = h_ref[...] + _dot(merged.astype(BF16), wo_ref[...])


def _merge(h, ya, yr, ga, gb, wa, wb, wo, tm):
    t = h.shape[0]
    row = lambda i: (i, 0)
    return pl.pallas_call(
        _merge_kernel,
        grid=(t // tm,),
        in_specs=[
            pl.BlockSpec((tm, D_MODEL), row),
            pl.BlockSpec((tm, A_WIDTH), row),
            pl.BlockSpec((tm, R_WIDTH), row),
            pl.BlockSpec((tm, D_MODEL), row),
            pl.BlockSpec((tm, D_MODEL), row),
            _const_spec((A_WIDTH, D_MODEL)),
            _const_spec((R_WIDTH, D_MODEL)),
            _const_spec((D_MODEL, D_MODEL)),
        ],
        out_specs=pl.BlockSpec((tm, D_MODEL), row),
        out_shape=jax.ShapeDtypeStruct((t, D_MODEL), F32),
        compiler_params=pltpu.CompilerParams(
            dimension_semantics=("parallel",), vmem_limit_bytes=VMEM_LIMIT),
        name="merge",
    )(h, ya, yr, ga, gb, wa, wb, wo)


def _ple_kernel(h_ref, p_ref, g_ref, wg_ref, wp_ref, gf_ref, o_ref):
    h = h_ref[...]
    u = _rms(h, g_ref[...]).astype(BF16)
    gate = jax.nn.sigmoid(_dot(u, wg_ref[...]))
    emb = _dot(p_ref[...].astype(BF16), wp_ref[...])
    o_ref[...] = _rms(h + gate * emb, gf_ref[...])


def _ple(h, p, g, wg, wp, gf, tm):
    t = h.shape[0]
    row = lambda i: (i, 0)
    return pl.pallas_call(
        _ple_kernel,
        grid=(t // tm,),
        in_specs=[
            pl.BlockSpec((tm, D_MODEL), row),
            pl.BlockSpec((tm, PLE_DIM), row),
            _const_spec((1, D_MODEL)),
            _const_spec((D_MODEL, D_MODEL)),
            _const_spec((PLE_DIM, D_MODEL)),
            _const_spec((1, D_MODEL)),
        ],
        out_specs=pl.BlockSpec((tm, D_MODEL), row),
        out_shape=jax.ShapeDtypeStruct((t, D_MODEL), F32),
        compiler_params=pltpu.CompilerParams(
            dimension_semantics=("parallel",), vmem_limit_bytes=VMEM_LIMIT),
        name="ple",
    )(h, p, g, wg, wp, gf)


def _token_tile(t):
    return 512 if t % 512 == 0 else 256


def kernel(x, p, positions, ffn1_norm, ffn1_w_gate, ffn1_w_up, ffn1_w_down, mix_norm, w_in, ret_gn, w_branch_a, w_branch_b, w_out, ffn2_norm, ffn2_w_gate, ffn2_w_up, ffn2_w_down, ple_norm, w_ple_gate, w_ple_proj, final_norm):
    batch, seq, _ = x.shape
    depth = p.shape[0]
    t = batch * seq
    tm = _token_tile(t)
    n_sel = min(TOPK_MAX, seq // 4)
    assert depth == 1, "the final norm is fused into the per-layer embedding step"
    assert seq % RET_BLOCK == 0 and seq % KEY_TILE == 0 and seq % DSA_QB == 0
    assert n_sel <= KEY_TILE and DSA_QB == KEY_TILE and KEY_TILE == 8 * WORD

    h = x.reshape(t, D_MODEL)
    pos3 = positions.reshape(t // tm, 1, tm)
    vec = lambda g: g.reshape(1, -1).astype(F32)

    for i in range(depth):
        w = w_in[i]
        wt = jnp.concatenate([
            w[:, _OFF_AQ:_OFF_AQ + A_WIDTH], w[:, _OFF_IQ:_OFF_IQ + IDX_HEADS * IDX_DIM],
            w[:, _OFF_RQ:_OFF_RQ + 512], w[:, _OFF_RK:_OFF_RK + 512],
            w[:, _OFF_AK:_OFF_AK + 64], w[:, _OFF_IK:_OFF_IK + 64],
            w[:, _OFF_AV:_OFF_AV + 64], w[:, _OFF_IW:_OFF_IW + IDX_HEADS]], axis=1).T.astype(BF16)
        wn = w[:, _OFF_RV:].astype(BF16)

        h = _ffn(h, vec(ffn1_norm[i]), ffn1_w_gate[i].astype(BF16), ffn1_w_up[i].astype(BF16),
                 ffn1_w_down[i].astype(BF16), tm)
        aqt, iqt, rq, rkt, kk, avt, iwt, rv, rg, ga, gb = _inproj(h, vec(mix_norm[i]), pos3, wt, wn, tm)
        ya = _dsa(kk, avt, aqt, iqt, iwt, batch, seq, n_sel)
        yr = _retention(rq, rkt, rv, rg, vec(ret_gn[i]), batch, seq)
        h = _merge(h, ya, yr, ga, gb, w_branch_a[i].astype(BF16), w_branch_b[i].astype(BF16),
                   w_out[i].astype(BF16), tm)
        h = _ffn(h, vec(ffn2_norm[i]), ffn2_w_gate[i].astype(BF16), ffn2_w_up[i].astype(BF16),
                 ffn2_w_down[i].astype(BF16), tm)
        h = _ple(h, p[i].reshape(t, PLE_DIM), vec(ple_norm[i]), w_ple_gate[i].astype(BF16),
                 w_ple_proj[i].astype(BF16), vec(final_norm), tm)
    return h.reshape(batch, seq, D_MODEL)
```

```python
import functools
import math

import jax
import jax.numpy as jnp
import numpy as np
from jax import lax
from jax.experimental import pallas as pl
from jax.experimental.pallas import tpu as pltpu

F32 = jnp.float32
BF16 = jnp.bfloat16
I32 = jnp.int32

D_MODEL = 1024
CHUNK = 64
ROPE_THETA = 10000.0
EPS = 1e-6
GN_EPS = 1e-5
NEG_BIG = -1e30
A_HEADS = 8
A_HEAD_DIM = 64
IDX_HEADS = 8
IDX_DIM = 64
TOPK_MAX = 256
A_WIDTH = A_HEADS * A_HEAD_DIM
R_HEADS = 4
R_QK_DIM = 128
R_V_DIM = 256
R_WIDTH = R_HEADS * R_V_DIM
D_FF = 2816
PLE_DIM = 256

_OFF_AQ = 0
_OFF_AK = _OFF_AQ + A_WIDTH
_OFF_AV = _OFF_AK + A_HEAD_DIM
_OFF_IQ = _OFF_AV + A_HEAD_DIM
_OFF_IK = _OFF_IQ + IDX_HEADS * IDX_DIM
_OFF_IW = _OFF_IK + IDX_DIM
_OFF_RQ = _OFF_IW + IDX_HEADS
_OFF_RK = _OFF_RQ + R_HEADS * R_QK_DIM
_OFF_RV = _OFF_RK + R_HEADS * R_QK_DIM
_N_TOKEN_MAJOR = 2 * R_WIDTH + 2 * D_MODEL

_T_AQ = 0
_T_IQ = _T_AQ + A_WIDTH
_T_RQ = _T_IQ + IDX_HEADS * IDX_DIM
_T_RK = _T_RQ + R_HEADS * R_QK_DIM
_T_AK = _T_RK + R_HEADS * R_QK_DIM
_T_IK = _T_AK + A_HEAD_DIM
_T_AV = _T_IK + IDX_DIM
_T_IW = _T_AV + A_HEAD_DIM
_T_ROWS = _T_IW + IDX_HEADS

VMEM_LIMIT = 56 * 1024 * 1024
FF_CHUNK = 256
DSA_QB = 256
KEY_TILE = 256
KEY_HALF = KEY_TILE // 2
RET_BLOCK = 256

AV_ROWS = A_HEAD_DIM + 16
INT_MIN = -(2 ** 31)
WORD = 32
LOG2_E = math.log2(math.e)


def _const_spec(shape):
    nd = len(shape)
    return pl.BlockSpec(shape, lambda *_: (0,) * nd, pipeline_mode=pl.Buffered(1))


def _rms(x, g):
    ms = jnp.mean(x * x, axis=-1, keepdims=True)
    return x * lax.rsqrt(ms + EPS) * g


def _dot(a, b):
    return jnp.dot(a, b, preferred_element_type=F32)


def _swiglu(u, wg_ref, wu_ref, wd_ref):
    acc = jnp.zeros((u.shape[0], D_MODEL), F32)
    for c in range(D_FF // FF_CHUNK):
        sl = slice(c * FF_CHUNK, (c + 1) * FF_CHUNK)
        gate = _dot(u, wg_ref[:, sl])
        up = _dot(u, wu_ref[:, sl])
        act = (gate * jax.nn.sigmoid(gate) * up).astype(BF16)
        acc = acc + _dot(act, wd_ref[sl, :])
    return acc


def _ffn_kernel(h_ref, g_ref, wg_ref, wu_ref, wd_ref, o_ref):
    h = h_ref[...]
    o_ref[...] = h + 0.5 * _swiglu(_rms(h, g_ref[...]).astype(BF16), wg_ref, wu_ref, wd_ref)


def _ffn(h, g, wg, wu, wd, tm):
    t = h.shape[0]
    return pl.pallas_call(
        _ffn_kernel,
        grid=(t // tm,),
        in_specs=[
            pl.BlockSpec((tm, D_MODEL), lambda i: (i, 0)),
            _const_spec((1, D_MODEL)),
            _const_spec((D_MODEL, D_FF)),
            _const_spec((D_MODEL, D_FF)),
            _const_spec((D_FF, D_MODEL)),
        ],
        out_specs=pl.BlockSpec((tm, D_MODEL), lambda i: (i, 0)),
        out_shape=jax.ShapeDtypeStruct((t, D_MODEL), F32),
        compiler_params=pltpu.CompilerParams(
            dimension_semantics=("parallel",), vmem_limit_bytes=VMEM_LIMIT),
        name="ffn",
    )(h, g, wg, wu, wd)


def _rope_rows(t1, t2, cos, sin):
    return t1 * cos - t2 * sin, t2 * cos + t1 * sin


def _inproj_kernel(h_ref, g_ref, pos_ref, wt_ref, wn_ref,
                   aqt_ref, iqt_ref, rq_ref, rkt_ref, kk_ref, avt_ref, iwt_ref,
                   rv_ref, rg_ref, ga_ref, gb_ref):
    tm = h_ref.shape[0]
    u = _rms(h_ref[...], g_ref[...]).astype(BF16)

    zn = _dot(u, wn_ref[...])
    rv_ref[...] = zn[:, 0:R_WIDTH].astype(BF16)
    rg_ref[...] = zn[:, R_WIDTH:2 * R_WIDTH].astype(BF16)
    ga_ref[...] = zn[:, 2 * R_WIDTH:2 * R_WIDTH + D_MODEL].astype(BF16)
    gb_ref[...] = zn[:, 2 * R_WIDTH + D_MODEL:].astype(BF16)

    zt = lax.dot_general(wt_ref[...], u, (((1,), (1,)), ((), ())),
                         preferred_element_type=F32)

    pos = pos_ref[0].astype(F32)

    def angles(half):
        i = lax.broadcasted_iota(I32, (half, tm), 0).astype(F32)
        inv = jnp.exp(i * (-math.log(ROPE_THETA) / half))
        ang = pos * inv
        return jnp.cos(ang), jnp.sin(ang)

    cos64, sin64 = angles(A_HEAD_DIM // 2)
    cos128, sin128 = angles(R_QK_DIM // 2)

    def rope_head(row0, dim, cos, sin):
        half = dim // 2
        return _rope_rows(zt[row0:row0 + half], zt[row0 + half:row0 + dim], cos, sin)

    for h in range(A_HEADS):
        o1, o2 = rope_head(_T_AQ + h * A_HEAD_DIM, A_HEAD_DIM, cos64, sin64)
        r = h * A_HEAD_DIM
        aqt_ref[r:r + 32, :] = (o1 * (A_HEAD_DIM ** -0.5 * LOG2_E)).astype(BF16)
        aqt_ref[r + 32:r + 64, :] = (o2 * (A_HEAD_DIM ** -0.5 * LOG2_E)).astype(BF16)
    for h in range(IDX_HEADS):
        o1, o2 = rope_head(_T_IQ + h * IDX_DIM, IDX_DIM, cos64, sin64)
        r = h * IDX_DIM
        iqt_ref[r:r + 32, :] = (o1 * (IDX_DIM ** -0.5)).astype(BF16)
        iqt_ref[r + 32:r + 64, :] = (o2 * (IDX_DIM ** -0.5)).astype(BF16)

    for h in range(R_HEADS):
        o1, o2 = rope_head(_T_RQ + h * R_QK_DIM, R_QK_DIM, cos128, sin128)
        q = jnp.concatenate([o1, o2], axis=0)
        rq_ref[:, h * R_QK_DIM:(h + 1) * R_QK_DIM] = q.T.astype(BF16)
        k1, k2 = rope_head(_T_RK + h * R_QK_DIM, R_QK_DIM, cos128, sin128)
        r = h * R_QK_DIM
        rkt_ref[r:r + 64, :] = (k1 * (R_QK_DIM ** -0.5)).astype(BF16)
        rkt_ref[r + 64:r + 128, :] = (k2 * (R_QK_DIM ** -0.5)).astype(BF16)

    a1, a2 = rope_head(_T_AK, A_HEAD_DIM, cos64, sin64)
    i1, i2 = rope_head(_T_IK, IDX_DIM, cos64, sin64)
    kk = jnp.concatenate([a1, a2, i1, i2], axis=0)
    kk_ref[...] = kk.T.astype(BF16)

    avt_ref[0:A_HEAD_DIM, :] = zt[_T_AV:_T_AV + A_HEAD_DIM].astype(BF16)
    avt_ref[A_HEAD_DIM:AV_ROWS, :] = jnp.ones((AV_ROWS - A_HEAD_DIM, tm), BF16)
    iwt_ref[...] = zt[_T_IW:_T_IW + IDX_HEADS] * (IDX_HEADS ** -0.5)


def _inproj(h, g, pos3, wt, wn, tm):
    t = h.shape[0]
    nt = t // tm
    row = lambda i: (i, 0)
    col = lambda i: (0, i)
    outs = [
        (jax.ShapeDtypeStruct((A_WIDTH, t), BF16), pl.BlockSpec((A_WIDTH, tm), col)),
        (jax.ShapeDtypeStruct((A_WIDTH, t), BF16), pl.BlockSpec((A_WIDTH, tm), col)),
        (jax.ShapeDtypeStruct((t, 512), BF16), pl.BlockSpec((tm, 512), row)),
        (jax.ShapeDtypeStruct((512, t), BF16), pl.BlockSpec((512, tm), col)),
        (jax.ShapeDtypeStruct((t, 128), BF16), pl.BlockSpec((tm, 128), row)),
        (jax.ShapeDtypeStruct((AV_ROWS, t), BF16), pl.BlockSpec((AV_ROWS, tm), col)),
        (jax.ShapeDtypeStruct((IDX_HEADS, t), F32), pl.BlockSpec((IDX_HEADS, tm), col)),
        (jax.ShapeDtypeStruct((t, R_WIDTH), BF16), pl.BlockSpec((tm, R_WIDTH), row)),
        (jax.ShapeDtypeStruct((t, R_WIDTH), BF16), pl.BlockSpec((tm, R_WIDTH), row)),
        (jax.ShapeDtypeStruct((t, D_MODEL), BF16), pl.BlockSpec((tm, D_MODEL), row)),
        (jax.ShapeDtypeStruct((t, D_MODEL), BF16), pl.BlockSpec((tm, D_MODEL), row)),
    ]
    return pl.pallas_call(
        _inproj_kernel,
        grid=(nt,),
        in_specs=[
            pl.BlockSpec((tm, D_MODEL), row),
            _const_spec((1, D_MODEL)),
            pl.BlockSpec((1, 1, tm), lambda i: (i, 0, 0)),
            _const_spec((_T_ROWS, D_MODEL)),
            _const_spec((D_MODEL, _N_TOKEN_MAJOR)),
        ],
        out_specs=[o[1] for o in outs],
        out_shape=[o[0] for o in outs],
        compiler_params=pltpu.CompilerParams(
            dimension_semantics=("parallel",), vmem_limit_bytes=VMEM_LIMIT),
        name="inproj",
    )(h, g, pos3, wt, wn)


def _tree_or(parts):
    while len(parts) > 1:
        parts = [parts[i] | parts[i + 1] for i in range(0, len(parts) - 1, 2)] + (
            [parts[-1]] if len(parts) % 2 else [])
    return parts[0]


_BT_MASKS = {16: 0x0000FFFF, 8: 0x00FF00FF, 4: 0x0F0F0F0F, 2: 0x33333333, 1: 0x55555555}


def _bit_transpose32(rows):
    a = list(rows)
    j = 16
    while j:
        mask = _BT_MASKS[j]
        k = 0
        while k < WORD:
            t = (a[k] ^ (a[k + j] >> j)) & mask
            a[k] = a[k] ^ t
            a[k + j] = a[k + j] ^ (t << j)
            k = (k + j + 1) & ~j
        j >>= 1
    return a


def _word_bit(m):
    v = 1 << (WORD - 1 - m)
    return v - (1 << WORD) if v >= (1 << (WORD - 1)) else v


def _dsa_kernel(kk_ref, avt_ref, aqt_ref, iqt_ref, iwt_ref, o_ref,
                lg_ref, planes_ref, posp_ref, valid_ref, sel_ref, qa_ref, qi_ref, acc_ref, ml_ref,
                *, n_sel, seq):
    j = pl.program_id(1)
    nq = DSA_QB
    n_tiles = j + 1
    words = seq // WORD
    wpt = KEY_TILE // WORD
    pos_bits = (seq - 1).bit_length()

    @pl.when((pl.program_id(0) == 0) & (j == 0))
    def _():
        qa_ref[64:128, :] = jnp.zeros((64, A_HEADS * nq), BF16)
        qi_ref[0:64, :] = jnp.zeros((64, A_HEADS * nq), BF16)
        planes_ref[...] = jnp.zeros(planes_ref.shape, I32)
        valid_ref[...] = jnp.zeros(valid_ref.shape, I32)
        for t in range(seq // KEY_TILE):
            sub = lax.broadcasted_iota(I32, (wpt, nq), 0)
            p = _bit_transpose32([(seq - 1 - t * KEY_TILE - m * wpt) - sub for m in range(WORD)])
            for i in range(pos_bits):
                posp_ref[i, t * wpt:(t + 1) * wpt, :] = p[WORD - pos_bits + i]

    for h in range(A_HEADS):
        cs = slice(h * nq, (h + 1) * nq)
        qa_ref[0:64, cs] = aqt_ref[h * 64:(h + 1) * 64, :]
        qi_ref[64:128, cs] = iqt_ref[h * 64:(h + 1) * 64, :]

    lane = lax.broadcasted_iota(I32, (1, nq), 1)
    vis_end = (j * nq + (lane // CHUNK + 1) * CHUNK)
    w = iwt_ref[...]

    def tile_rows(t):
        return pl.ds(pl.multiple_of(t * KEY_TILE, KEY_TILE), KEY_TILE)

    def word_rows(t):
        return pl.ds(pl.multiple_of(t * wpt, wpt), wpt)

    def score_tile(t, carry):
        keys, valid = [], []
        for half in range(2):
            r0 = pl.multiple_of(t * KEY_TILE + half * KEY_HALF, KEY_HALF)
            kt = kk_ref[pl.ds(r0, KEY_HALF), :]
            sc = None
            for h in range(IDX_HEADS):
                rel = jnp.maximum(_dot(kt, qi_ref[:, h * nq:(h + 1) * nq]), 0.0)
                part = rel * w[h:h + 1, :]
                sc = part if sc is None else sc + part
            sc = jnp.where(sc == 0.0, 0.0, sc)
            spos = r0 + lax.broadcasted_iota(I32, (KEY_HALF, nq), 0)
            sc = jnp.where(spos < vis_end, sc, NEG_BIG)
            bits = pltpu.bitcast(sc, I32)
            keys.append(bits ^ ((bits >> 31) | jnp.int32(INT_MIN)))
            valid.append(sc > 0.5 * NEG_BIG)

        kt_all = kk_ref[tile_rows(t), :]
        for h in range(A_HEADS):
            cs = slice(h * nq, (h + 1) * nq)
            lg_ref[tile_rows(t), cs] = _dot(kt_all, qa_ref[:, cs])

        def member(parts, m):
            r = (m % 16) * 8
            return parts[m // 16][r:r + 8, :]

        planes = _bit_transpose32([member(keys, m) for m in range(WORD)])
        for i in range(WORD):
            planes_ref[i, word_rows(t), :] = planes[i]
        valid_ref[word_rows(t), :] = _tree_or(
            [jnp.where(member(valid, m), jnp.int32(_word_bit(m)), jnp.int32(0)) for m in range(WORD)])
        return carry

    lax.fori_loop(0, n_tiles, score_tile, 0)

    wrow = lax.broadcasted_iota(I32, (words, nq), 0)
    state = (jnp.where(wrow < n_tiles * wpt, jnp.int32(-1), jnp.int32(0)),
             jnp.zeros((words, nq), I32),
             jnp.zeros((1, nq), I32))

    def radix_step(plane, state):
        alive, sel, above = state
        ones = alive & plane
        c = jnp.sum(lax.population_count(ones), axis=0, keepdims=True)
        keep_ones = (above + c) >= n_sel
        return (jnp.where(keep_ones, ones, alive ^ ones),
                jnp.where(keep_ones, sel, sel | ones),
                jnp.where(keep_ones, above, above + c))

    state = lax.fori_loop(0, WORD, lambda i, s: radix_step(planes_ref[i], s), state)
    state = lax.fori_loop(0, pos_bits, lambda i, s: radix_step(posp_ref[i], s), state)
    alive, sel, _ = state
    sel_ref[...] = (sel | alive) & valid_ref[...]

    acc_ref[...] = jnp.zeros(acc_ref.shape, F32)
    ml_ref[0:1, :] = jnp.full((1, A_HEADS * nq), 0.5 * NEG_BIG, F32)

    def attn_tile(t, carry):
        rows = tile_rows(t)
        vt = avt_ref[:, rows]
        selw = sel_ref[word_rows(t), :]
        bias = jnp.concatenate(
            [jnp.where((selw << m) < 0, 0.0, NEG_BIG) for m in range(WORD)], axis=0)
        for h in range(A_HEADS):
            cs = slice(h * nq, (h + 1) * nq)
            lg = lg_ref[rows, cs] + bias
            m_old = ml_ref[0:1, cs]
            m_new = jnp.maximum(m_old, jnp.max(lg, axis=0, keepdims=True))
            alpha = jnp.exp2(m_old - m_new)
            p = jnp.exp2(lg - m_new)
            ml_ref[0:1, cs] = m_new
            acc_ref[:, cs] = acc_ref[:, cs] * alpha + _dot(vt, p.astype(BF16))
        return carry

    lax.fori_loop(0, n_tiles, attn_tile, 0)

    out_t = acc_ref[0:A_HEAD_DIM, :] / acc_ref[A_HEAD_DIM:A_HEAD_DIM + 1, :]
    for hp in range(A_HEADS // 2):
        blk = jnp.concatenate(
            [out_t[:, (2 * hp) * nq:(2 * hp + 1) * nq], out_t[:, (2 * hp + 1) * nq:(2 * hp + 2) * nq]], axis=0)
        o_ref[:, hp * 128:(hp + 1) * 128] = blk.T.astype(o_ref.dtype)


def _dsa(kk, avt, aqt, iqt, iwt, batch, seq, n_sel):
    t = batch * seq
    nqb = seq // DSA_QB
    qblk = lambda b, j: (0, b * nqb + j)
    return pl.pallas_call(
        functools.partial(_dsa_kernel, n_sel=n_sel, seq=seq),
        grid=(batch, nqb),
        in_specs=[
            pl.BlockSpec((seq, 128), lambda b, j: (b, 0)),
            pl.BlockSpec((AV_ROWS, seq), lambda b, j: (0, b)),
            pl.BlockSpec((A_WIDTH, DSA_QB), qblk),
            pl.BlockSpec((A_WIDTH, DSA_QB), qblk),
            pl.BlockSpec((IDX_HEADS, DSA_QB), qblk),
        ],
        out_specs=pl.BlockSpec((DSA_QB, A_WIDTH), lambda b, j: (b * nqb + j, 0)),
        out_shape=jax.ShapeDtypeStruct((t, A_WIDTH), BF16),
        scratch_shapes=[
            pltpu.VMEM((seq, A_HEADS * DSA_QB), F32),
            pltpu.VMEM((WORD, seq // WORD, DSA_QB), I32),
            pltpu.VMEM(((seq - 1).bit_length(), seq // WORD, DSA_QB), I32),
            pltpu.VMEM((seq // WORD, DSA_QB), I32),
            pltpu.VMEM((seq // WORD, DSA_QB), I32),
            pltpu.VMEM((128, A_HEADS * DSA_QB), BF16),
            pltpu.VMEM((128, A_HEADS * DSA_QB), BF16),
            pltpu.VMEM((AV_ROWS, A_HEADS * DSA_QB), F32),
            pltpu.VMEM((8, A_HEADS * DSA_QB), F32),
        ],
        compiler_params=pltpu.CompilerParams(
            dimension_semantics=("arbitrary", "arbitrary"), vmem_limit_bytes=VMEM_LIMIT),
        name="dsa",
    )(kk, avt, aqt, iqt, iwt)


def _log_gamma(h):
    return math.log1p(-(2.0 ** (-5.0 - h)))


def _retention_kernel(rq_ref, rkt_ref, rv_ref, rg_ref, gn_ref, o_ref,
                      state_ref, dmat_ref, dq_ref):
    nb = RET_BLOCK
    first = (pl.program_id(0) == 0) & (pl.program_id(1) == 0)

    @pl.when(first)
    def _():
        i = lax.broadcasted_iota(I32, (nb, nb), 0)
        jj = lax.broadcasted_iota(I32, (nb, nb), 1)
        dist = jnp.abs(i - jj).astype(F32)
        seen = (jj // CHUNK) <= (i // CHUNK)
        for h in range(R_HEADS):
            lg = _log_gamma(h)
            dmat_ref[h] = jnp.where(seen, jnp.exp(lg * dist), 0.0)
            dq_ref[h] = jnp.exp(lg * (i.astype(F32) + 1.0))

    @pl.when(pl.program_id(1) == 0)
    def _():
        state_ref[...] = jnp.zeros(state_ref.shape, F32)

    jrow = lax.broadcasted_iota(I32, (1, nb), 1).astype(F32)
    for h in range(R_HEADS):
        lg = _log_gamma(h)
        q = rq_ref[:, h * R_QK_DIM:(h + 1) * R_QK_DIM]
        kt = rkt_ref[h * R_QK_DIM:(h + 1) * R_QK_DIM, :]
        v = rv_ref[:, h * R_V_DIM:(h + 1) * R_V_DIM]
        state = state_ref[h]
        s = _dot(q, kt) * dmat_ref[h]
        y = _dot(s.astype(BF16), v) + _dot(q, state.astype(BF16)) * dq_ref[h]
        dk = jnp.exp(lg * (nb - 1.0 - jrow))
        kd = (kt.astype(F32) * dk).astype(BF16)
        state_ref[h] = state * math.exp(lg * nb) + _dot(kd, v)
        mean = jnp.mean(y, axis=-1, keepdims=True)
        yc = y - mean
        var = jnp.mean(yc * yc, axis=-1, keepdims=True)
        yn = yc * lax.rsqrt(var + GN_EPS) * gn_ref[:, h * R_V_DIM:(h + 1) * R_V_DIM]
        gate = rg_ref[:, h * R_V_DIM:(h + 1) * R_V_DIM].astype(F32)
        o_ref[:, h * R_V_DIM:(h + 1) * R_V_DIM] = (yn * (gate * jax.nn.sigmoid(gate))).astype(o_ref.dtype)


def _retention(rq, rkt, rv, rg, gn, batch, seq):
    t = batch * seq
    nblk = seq // RET_BLOCK
    tok = lambda b, n: (b * nblk + n, 0)
    return pl.pallas_call(
        _retention_kernel,
        grid=(batch, nblk),
        in_specs=[
            pl.BlockSpec((RET_BLOCK, 512), tok),
            pl.BlockSpec((512, RET_BLOCK), lambda b, n: (0, b * nblk + n)),
            pl.BlockSpec((RET_BLOCK, R_WIDTH), tok),
            pl.BlockSpec((RET_BLOCK, R_WIDTH), tok),
            pl.BlockSpec((1, R_WIDTH), lambda b, n: (0, 0)),
        ],
        out_specs=pl.BlockSpec((RET_BLOCK, R_WIDTH), tok),
        out_shape=jax.ShapeDtypeStruct((t, R_WIDTH), BF16),
        scratch_shapes=[
            pltpu.VMEM((R_HEADS, R_QK_DIM, R_V_DIM), F32),
            pltpu.VMEM((R_HEADS, RET_BLOCK, RET_BLOCK), F32),
            pltpu.VMEM((R_HEADS, RET_BLOCK, RET_BLOCK), F32),
        ],
        compiler_params=pltpu.CompilerParams(
            dimension_semantics=("arbitrary", "arbitrary"), vmem_limit_bytes=VMEM_LIMIT),
        name="retention",
    )(rq, rkt, rv, rg, gn)


def _tail_kernel(h_ref, ya_ref, yr_ref, ga_ref, gb_ref, p_ref,
                 wa_ref, wb_ref, wo_ref, g2_ref, wg_ref, wu_ref, wd_ref,
                 gp_ref, wpg_ref, wpp_ref, gf_ref, o_ref):
    a = _dot(ya_ref[...], wa_ref[...])
    b = _dot(yr_ref[...], wb_ref[...])
    merged = (jax.nn.sigmoid(ga_ref[...].astype(F32)) * a
              + jax.nn.sigmoid(gb_ref[...].astype(F32)) * b)
    h = h_ref[...] + _dot(merged.astype(BF16), wo_ref[...])
    h = h + 0.5 * _swiglu(_rms(h, g2_ref[...]).astype(BF16), wg_ref, wu_ref, wd_ref)
    gate = jax.nn.sigmoid(_dot(_rms(h, gp_ref[...]).astype(BF16), wpg_ref[...]))
    emb = _dot(p_ref[...].astype(BF16), wpp_ref[...])
    o_ref[...] = _rms(h + gate * emb, gf_ref[...])


def _tail(h, ya, yr, ga, gb, p, wa, wb, wo, g2, wg, wu, wd, gp, wpg, wpp, gf, tm):
    t = h.shape[0]
    row = lambda i: (i, 0)
    return pl.pallas_call(
        _tail_kernel,
        grid=(t // tm,),
        in_specs=[
            pl.BlockSpec((tm, D_MODEL), row),
            pl.BlockSpec((tm, A_WIDTH), row),
            pl.BlockSpec((tm, R_WIDTH), row),
            pl.BlockSpec((tm, D_MODEL), row),
            pl.BlockSpec((tm, D_MODEL), row),
            pl.BlockSpec((tm, PLE_DIM), row),
            _const_spec((A_WIDTH, D_MODEL)),
            _const_spec((R_WIDTH, D_MODEL)),
            _const_spec((D_MODEL, D_MODEL)),
            _const_spec((1, D_MODEL)),
            _const_spec((D_MODEL, D_FF)),
            _const_spec((D_MODEL, D_FF)),
            _const_spec((D_FF, D_MODEL)),
            _const_spec((1, D_MODEL)),
            _const_spec((D_MODEL, D_MODEL)),
            _const_spec((PLE_DIM, D_MODEL)),
            _const_spec((1, D_MODEL)),
        ],
        out_specs=pl.BlockSpec((tm, D_MODEL), row),
        out_shape=jax.ShapeDtypeStruct((t, D_MODEL), F32),
        compiler_params=pltpu.CompilerParams(
            dimension_semantics=("parallel",), vmem_limit_bytes=VMEM_LIMIT),
        name="tail",
    )(h, ya, yr, ga, gb, p, wa, wb, wo, g2, wg, wu, wd, gp, wpg, wpp, gf)


def _token_tile(t):
    return 512 if t % 512 == 0 else 256


def kernel(x, p, positions, ffn1_norm, ffn1_w_gate, ffn1_w_up, ffn1_w_down, mix_norm, w_in, ret_gn, w_branch_a, w_branch_b, w_out, ffn2_norm, ffn2_w_gate, ffn2_w_up, ffn2_w_down, ple_norm, w_ple_gate, w_ple_proj, final_norm):
    batch, seq, _ = x.shape
    depth = p.shape[0]
    t = batch * seq
    tm = _token_tile(t)
    n_sel = min(TOPK_MAX, seq // 4)
    assert depth == 1, "the final norm is fused into the per-layer embedding step"
    assert seq % RET_BLOCK == 0 and seq % KEY_TILE == 0 and seq % DSA_QB == 0
    assert n_sel <= KEY_TILE and DSA_QB == KEY_TILE and KEY_TILE == 8 * WORD

    h = x.reshape(t, D_MODEL)
    pos3 = positions.reshape(t // tm, 1, tm)
    vec = lambda g: g.reshape(1, -1).astype(F32)

    for i in range(depth):
        w = w_in[i]
        wt = jnp.concatenate([
            w[:, _OFF_AQ:_OFF_AQ + A_WIDTH], w[:, _OFF_IQ:_OFF_IQ + IDX_HEADS * IDX_DIM],
            w[:, _OFF_RQ:_OFF_RQ + 512], w[:, _OFF_RK:_OFF_RK + 512],
            w[:, _OFF_AK:_OFF_AK + 64], w[:, _OFF_IK:_OFF_IK + 64],
            w[:, _OFF_AV:_OFF_AV + 64], w[:, _OFF_IW:_OFF_IW + IDX_HEADS]], axis=1).T.astype(BF16)
        wn = w[:, _OFF_RV:].astype(BF16)

        h = _ffn(h, vec(ffn1_norm[i]), ffn1_w_gate[i].astype(BF16), ffn1_w_up[i].astype(BF16),
                 ffn1_w_down[i].astype(BF16), tm)
        aqt, iqt, rq, rkt, kk, avt, iwt, rv, rg, ga, gb = _inproj(h, vec(mix_norm[i]), pos3, wt, wn, tm)
        ya = _dsa(kk, avt, aqt, iqt, iwt, batch, seq, n_sel)
        yr = _retention(rq, rkt, rv, rg, vec(ret_gn[i]), batch, seq)
        h = _tail(h, ya, yr, ga, gb, p[i].reshape(t, PLE_DIM),
                  w_branch_a[i].astype(BF16), w_branch_b[i].astype(BF16), w_out[i].astype(BF16),
                  vec(ffn2_norm[i]), ffn2_w_gate[i].astype(BF16), ffn2_w_up[i].astype(BF16),
                  ffn2_w_down[i].astype(BF16),
                  vec(ple_norm[i]), w_ple_gate[i].astype(BF16), w_ple_proj[i].astype(BF16),
                  vec(final_norm), tm)
    return h.reshape(batch, seq, D_MODEL)
```

```python
import functools
import math

import jax
import jax.numpy as jnp
import numpy as np
from jax import lax
from jax.experimental import pallas as pl
from jax.experimental.pallas import tpu as pltpu

F32 = jnp.float32
BF16 = jnp.bfloat16
I32 = jnp.int32

D_MODEL = 1024
CHUNK = 64
ROPE_THETA = 10000.0
EPS = 1e-6
GN_EPS = 1e-5
NEG_BIG = -1e30
A_HEADS = 8
A_HEAD_DIM = 64
IDX_HEADS = 8
IDX_DIM = 64
TOPK_MAX = 256
A_WIDTH = A_HEADS * A_HEAD_DIM
R_HEADS = 4
R_QK_DIM = 128
R_V_DIM = 256
R_WIDTH = R_HEADS * R_V_DIM
D_FF = 2816
PLE_DIM = 256

_OFF_AQ = 0
_OFF_AK = _OFF_AQ + A_WIDTH
_OFF_AV = _OFF_AK + A_HEAD_DIM
_OFF_IQ = _OFF_AV + A_HEAD_DIM
_OFF_IK = _OFF_IQ + IDX_HEADS * IDX_DIM
_OFF_IW = _OFF_IK + IDX_DIM
_OFF_RQ = _OFF_IW + IDX_HEADS
_OFF_RK = _OFF_RQ + R_HEADS * R_QK_DIM
_OFF_RV = _OFF_RK + R_HEADS * R_QK_DIM
_N_TOKEN_MAJOR = 2 * R_WIDTH + 2 * D_MODEL

_T_AQ = 0
_T_IQ = _T_AQ + A_WIDTH
_T_RQ = _T_IQ + IDX_HEADS * IDX_DIM
_T_RK = _T_RQ + R_HEADS * R_QK_DIM
_T_AK = _T_RK + R_HEADS * R_QK_DIM
_T_IK = _T_AK + A_HEAD_DIM
_T_AV = _T_IK + IDX_DIM
_T_IW = _T_AV + A_HEAD_DIM
_T_ROWS = _T_IW + IDX_HEADS

VMEM_LIMIT = 56 * 1024 * 1024
FF_CHUNK = 256
DSA_QB = 256
KEY_TILE = 256
KEY_HALF = KEY_TILE // 2
RET_BLOCK = 256

AV_ROWS = A_HEAD_DIM + 16
INT_MIN = -(2 ** 31)
WORD = 32
LOG2_E = math.log2(math.e)
DENOM_MIN = 2.0 ** -100


def _const_spec(shape):
    nd = len(shape)
    return pl.BlockSpec(shape, lambda *_: (0,) * nd, pipeline_mode=pl.Buffered(1))


def _rms(x, g):
    ms = jnp.mean(x * x, axis=-1, keepdims=True)
    return x * lax.rsqrt(ms + EPS) * g


def _dot(a, b):
    return jnp.dot(a, b, preferred_element_type=F32)


def _swiglu(u, wg_ref, wu_ref, wd_ref):
    acc = jnp.zeros((u.shape[0], D_MODEL), F32)
    for c in range(D_FF // FF_CHUNK):
        sl = slice(c * FF_CHUNK, (c + 1) * FF_CHUNK)
        gate = _dot(u, wg_ref[:, sl])
        up = _dot(u, wu_ref[:, sl])
        act = (gate * jax.nn.sigmoid(gate) * up).astype(BF16)
        acc = acc + _dot(act, wd_ref[sl, :])
    return acc


def _ffn_kernel(h_ref, g_ref, wg_ref, wu_ref, wd_ref, o_ref):
    h = h_ref[...]
    o_ref[...] = h + 0.5 * _swiglu(_rms(h, g_ref[...]).astype(BF16), wg_ref, wu_ref, wd_ref)


def _ffn(h, g, wg, wu, wd, tm):
    t = h.shape[0]
    return pl.pallas_call(
        _ffn_kernel,
        grid=(t // tm,),
        in_specs=[
            pl.BlockSpec((tm, D_MODEL), lambda i: (i, 0)),
            _const_spec((1, D_MODEL)),
            _const_spec((D_MODEL, D_FF)),
            _const_spec((D_MODEL, D_FF)),
            _const_spec((D_FF, D_MODEL)),
        ],
        out_specs=pl.BlockSpec((tm, D_MODEL), lambda i: (i, 0)),
        out_shape=jax.ShapeDtypeStruct((t, D_MODEL), F32),
        compiler_params=pltpu.CompilerParams(
            dimension_semantics=("parallel",), vmem_limit_bytes=VMEM_LIMIT),
        name="ffn",
    )(h, g, wg, wu, wd)


def _rope_rows(t1, t2, cos, sin):
    return t1 * cos - t2 * sin, t2 * cos + t1 * sin


def _inproj_kernel(h_ref, g_ref, pos_ref, wt_ref, wn_ref,
                   aqt_ref, iqt_ref, rq_ref, rkt_ref, kk_ref, avt_ref, iwt_ref,
                   rv_ref, rg_ref, ga_ref, gb_ref):
    tm = h_ref.shape[0]
    u = _rms(h_ref[...], g_ref[...]).astype(BF16)

    zn = _dot(u, wn_ref[...])
    rv_ref[...] = zn[:, 0:R_WIDTH].astype(BF16)
    rg_ref[...] = zn[:, R_WIDTH:2 * R_WIDTH].astype(BF16)
    ga_ref[...] = zn[:, 2 * R_WIDTH:2 * R_WIDTH + D_MODEL].astype(BF16)
    gb_ref[...] = zn[:, 2 * R_WIDTH + D_MODEL:].astype(BF16)

    zt = lax.dot_general(wt_ref[...], u, (((1,), (1,)), ((), ())),
                         preferred_element_type=F32)

    pos = pos_ref[0].astype(F32)

    def angles(half):
        i = lax.broadcasted_iota(I32, (half, tm), 0).astype(F32)
        inv = jnp.exp(i * (-math.log(ROPE_THETA) / half))
        ang = pos * inv
        return jnp.cos(ang), jnp.sin(ang)

    cos64, sin64 = angles(A_HEAD_DIM // 2)
    cos128, sin128 = angles(R_QK_DIM // 2)

    def rope_head(row0, dim, cos, sin):
        half = dim // 2
        return _rope_rows(zt[row0:row0 + half], zt[row0 + half:row0 + dim], cos, sin)

    for h in range(A_HEADS):
        o1, o2 = rope_head(_T_AQ + h * A_HEAD_DIM, A_HEAD_DIM, cos64, sin64)
        r = h * A_HEAD_DIM
        aqt_ref[r:r + 32, :] = (o1 * (A_HEAD_DIM ** -0.5 * LOG2_E)).astype(BF16)
        aqt_ref[r + 32:r + 64, :] = (o2 * (A_HEAD_DIM ** -0.5 * LOG2_E)).astype(BF16)
    for h in range(IDX_HEADS):
        o1, o2 = rope_head(_T_IQ + h * IDX_DIM, IDX_DIM, cos64, sin64)
        r = h * IDX_DIM
        iqt_ref[r:r + 32, :] = (o1 * (IDX_DIM ** -0.5)).astype(BF16)
        iqt_ref[r + 32:r + 64, :] = (o2 * (IDX_DIM ** -0.5)).astype(BF16)

    for h in range(R_HEADS):
        o1, o2 = rope_head(_T_RQ + h * R_QK_DIM, R_QK_DIM, cos128, sin128)
        q = jnp.concatenate([o1, o2], axis=0)
        rq_ref[:, h * R_QK_DIM:(h + 1) * R_QK_DIM] = q.T.astype(BF16)
        k1, k2 = rope_head(_T_RK + h * R_QK_DIM, R_QK_DIM, cos128, sin128)
        r = h * R_QK_DIM
        rkt_ref[r:r + 64, :] = (k1 * (R_QK_DIM ** -0.5)).astype(BF16)
        rkt_ref[r + 64:r + 128, :] = (k2 * (R_QK_DIM ** -0.5)).astype(BF16)

    a1, a2 = rope_head(_T_AK, A_HEAD_DIM, cos64, sin64)
    i1, i2 = rope_head(_T_IK, IDX_DIM, cos64, sin64)
    kk = jnp.concatenate([a1, a2, i1, i2], axis=0)
    kk_ref[...] = kk.T.astype(BF16)

    avt_ref[0:A_HEAD_DIM, :] = zt[_T_AV:_T_AV + A_HEAD_DIM].astype(BF16)
    avt_ref[A_HEAD_DIM:AV_ROWS, :] = jnp.ones((AV_ROWS - A_HEAD_DIM, tm), BF16)
    iwt_ref[...] = zt[_T_IW:_T_IW + IDX_HEADS] * (IDX_HEADS ** -0.5)


def _inproj(h, g, pos3, wt, wn, tm):
    t = h.shape[0]
    nt = t // tm
    row = lambda i: (i, 0)
    col = lambda i: (0, i)
    outs = [
        (jax.ShapeDtypeStruct((A_WIDTH, t), BF16), pl.BlockSpec((A_WIDTH, tm), col)),
        (jax.ShapeDtypeStruct((A_WIDTH, t), BF16), pl.BlockSpec((A_WIDTH, tm), col)),
        (jax.ShapeDtypeStruct((t, 512), BF16), pl.BlockSpec((tm, 512), row)),
        (jax.ShapeDtypeStruct((512, t), BF16), pl.BlockSpec((512, tm), col)),
        (jax.ShapeDtypeStruct((t, 128), BF16), pl.BlockSpec((tm, 128), row)),
        (jax.ShapeDtypeStruct((AV_ROWS, t), BF16), pl.BlockSpec((AV_ROWS, tm), col)),
        (jax.ShapeDtypeStruct((IDX_HEADS, t), F32), pl.BlockSpec((IDX_HEADS, tm), col)),
        (jax.ShapeDtypeStruct((t, R_WIDTH), BF16), pl.BlockSpec((tm, R_WIDTH), row)),
        (jax.ShapeDtypeStruct((t, R_WIDTH), BF16), pl.BlockSpec((tm, R_WIDTH), row)),
        (jax.ShapeDtypeStruct((t, D_MODEL), BF16), pl.BlockSpec((tm, D_MODEL), row)),
        (jax.ShapeDtypeStruct((t, D_MODEL), BF16), pl.BlockSpec((tm, D_MODEL), row)),
    ]
    return pl.pallas_call(
        _inproj_kernel,
        grid=(nt,),
        in_specs=[
            pl.BlockSpec((tm, D_MODEL), row),
            _const_spec((1, D_MODEL)),
            pl.BlockSpec((1, 1, tm), lambda i: (i, 0, 0)),
            _const_spec((_T_ROWS, D_MODEL)),
            _const_spec((D_MODEL, _N_TOKEN_MAJOR)),
        ],
        out_specs=[o[1] for o in outs],
        out_shape=[o[0] for o in outs],
        compiler_params=pltpu.CompilerParams(
            dimension_semantics=("parallel",), vmem_limit_bytes=VMEM_LIMIT),
        name="inproj",
    )(h, g, pos3, wt, wn)


def _tree_or(parts):
    while len(parts) > 1:
        parts = [parts[i] | parts[i + 1] for i in range(0, len(parts) - 1, 2)] + (
            [parts[-1]] if len(parts) % 2 else [])
    return parts[0]


_BT_MASKS = {16: 0x0000FFFF, 8: 0x00FF00FF, 4: 0x0F0F0F0F, 2: 0x33333333, 1: 0x55555555}


def _bit_transpose32(rows):
    a = list(rows)
    j = 16
    while j:
        mask = _BT_MASKS[j]
        k = 0
        while k < WORD:
            t = (a[k] ^ (a[k + j] >> j)) & mask
            a[k] = a[k] ^ t
            a[k + j] = a[k + j] ^ (t << j)
            k = (k + j + 1) & ~j
        j >>= 1
    return a


def _word_bit(m):
    v = 1 << (WORD - 1 - m)
    return v - (1 << WORD) if v >= (1 << (WORD - 1)) else v


def _dsa_kernel(kk_ref, avt_ref, aqt_ref, iqt_ref, iwt_ref, o_ref,
                lg_ref, planes_ref, posp_ref, valid_ref, sel_ref, qa_ref, qi_ref,
                acc_ref, ml_ref, *, n_sel, seq):
    j = pl.program_id(1)
    nq = DSA_QB
    n_tiles = j + 1
    words = seq // WORD
    wpt = KEY_TILE // WORD
    pos_bits = (seq - 1).bit_length()

    @pl.when((pl.program_id(0) == 0) & (j == 0))
    def _():
        qa_ref[64:128, :] = jnp.zeros((64, A_HEADS * nq), BF16)
        qi_ref[0:64, :] = jnp.zeros((64, A_HEADS * nq), BF16)
        planes_ref[...] = jnp.zeros(planes_ref.shape, I32)
        valid_ref[...] = jnp.zeros(valid_ref.shape, I32)
        for t in range(seq // KEY_TILE):
            sub = lax.broadcasted_iota(I32, (wpt, nq), 0)
            p = _bit_transpose32([(seq - 1 - t * KEY_TILE - m * wpt) - sub for m in range(WORD)])
            for i in range(pos_bits):
                posp_ref[i, t * wpt:(t + 1) * wpt, :] = p[WORD - pos_bits + i]

    for h in range(A_HEADS):
        cs = slice(h * nq, (h + 1) * nq)
        qa_ref[0:64, cs] = aqt_ref[h * 64:(h + 1) * 64, :]
        qi_ref[64:128, cs] = iqt_ref[h * 64:(h + 1) * 64, :]

    lane = lax.broadcasted_iota(I32, (1, nq), 1)
    vis_end = (j * nq + (lane // CHUNK + 1) * CHUNK)
    w = iwt_ref[...]

    def tile_rows(t):
        return pl.ds(pl.multiple_of(t * KEY_TILE, KEY_TILE), KEY_TILE)

    def word_rows(t):
        return pl.ds(pl.multiple_of(t * wpt, wpt), wpt)

    ml_ref[1:2, :] = jnp.full((1, A_HEADS * nq), 0.5 * NEG_BIG, F32)

    def score_tile(t, carry):
        keys, valid = [], []
        for half in range(2):
            r0 = pl.multiple_of(t * KEY_TILE + half * KEY_HALF, KEY_HALF)
            kt = kk_ref[pl.ds(r0, KEY_HALF), :]
            sc = None
            for h in range(IDX_HEADS):
                rel = jnp.maximum(_dot(kt, qi_ref[:, h * nq:(h + 1) * nq]), 0.0)
                part = rel * w[h:h + 1, :]
                sc = part if sc is None else sc + part
            sc = jnp.where(sc == 0.0, 0.0, sc)
            spos = r0 + lax.broadcasted_iota(I32, (KEY_HALF, nq), 0)
            sc = jnp.where(spos < vis_end, sc, NEG_BIG)
            bits = pltpu.bitcast(sc, I32)
            keys.append(bits ^ ((bits >> 31) | jnp.int32(INT_MIN)))
            valid.append(sc > 0.5 * NEG_BIG)

        kt_all = kk_ref[tile_rows(t), :]
        for h in range(A_HEADS):
            cs = slice(h * nq, (h + 1) * nq)
            lg = _dot(kt_all, qa_ref[:, cs])
            lg_ref[tile_rows(t), cs] = lg
            ml_ref[1:2, cs] = jnp.maximum(ml_ref[1:2, cs], jnp.max(lg, axis=0, keepdims=True))

        def member(parts, m):
            r = (m % 16) * 8
            return parts[m // 16][r:r + 8, :]

        planes = _bit_transpose32([member(keys, m) for m in range(WORD)])
        for i in range(WORD):
            planes_ref[i, word_rows(t), :] = planes[i]
        valid_ref[word_rows(t), :] = _tree_or(
            [jnp.where(member(valid, m), jnp.int32(_word_bit(m)), jnp.int32(0)) for m in range(WORD)])
        return carry

    lax.fori_loop(0, n_tiles, score_tile, 0)

    wrow = lax.broadcasted_iota(I32, (words, nq), 0)
    state = (jnp.where(wrow < n_tiles * wpt, jnp.int32(-1), jnp.int32(0)),
             jnp.zeros((words, nq), I32),
             jnp.zeros((1, nq), I32))

    def radix_step(plane, state):
        alive, sel, above = state
        ones = alive & plane
        c = jnp.sum(lax.population_count(ones), axis=0, keepdims=True)
        keep_ones = (above + c) >= n_sel
        return (jnp.where(keep_ones, ones, alive ^ ones),
                jnp.where(keep_ones, sel, sel | ones),
                jnp.where(keep_ones, above, above + c))

    state = lax.fori_loop(0, WORD, lambda i, s: radix_step(planes_ref[i], s), state)
    state = lax.fori_loop(0, pos_bits, lambda i, s: radix_step(posp_ref[i], s), state)
    alive, sel, _ = state
    sel_ref[...] = (sel | alive) & valid_ref[...]

    def masked_logits(t, cs, bias):
        return lg_ref[tile_rows(t), cs] + bias

    def mask_bias(t):
        selw = sel_ref[word_rows(t), :]
        return jnp.concatenate(
            [jnp.where((selw << m) < 0, 0.0, NEG_BIG) for m in range(WORD)], axis=0)

    def attend(shift_row):
        acc_ref[...] = jnp.zeros(acc_ref.shape, F32)

        def attn_tile(t, carry):
            vt = avt_ref[:, tile_rows(t)]
            bias = mask_bias(t)
            for h in range(A_HEADS):
                cs = slice(h * nq, (h + 1) * nq)
                p = jnp.exp2(masked_logits(t, cs, bias) - ml_ref[shift_row:shift_row + 1, cs])
                acc_ref[:, cs] = acc_ref[:, cs] + _dot(vt, p.astype(BF16))
            return carry

        lax.fori_loop(0, n_tiles, attn_tile, 0)

    attend(1)
    underflow = jnp.min(acc_ref[A_HEAD_DIM:A_HEAD_DIM + 1, :]) < DENOM_MIN

    @pl.when(underflow)
    def _():
        ml_ref[0:1, :] = jnp.full((1, A_HEADS * nq), 0.5 * NEG_BIG, F32)

        def max_tile(t, carry):
            bias = mask_bias(t)
            for h in range(A_HEADS):
                cs = slice(h * nq, (h + 1) * nq)
                ml_ref[0:1, cs] = jnp.maximum(
                    ml_ref[0:1, cs], jnp.max(masked_logits(t, cs, bias), axis=0, keepdims=True))
            return carry

        lax.fori_loop(0, n_tiles, max_tile, 0)
        attend(0)

    out_t = acc_ref[0:A_HEAD_DIM, :] / acc_ref[A_HEAD_DIM:A_HEAD_DIM + 1, :]
    for hp in range(A_HEADS // 2):
        blk = jnp.concatenate(
            [out_t[:, (2 * hp) * nq:(2 * hp + 1) * nq], out_t[:, (2 * hp + 1) * nq:(2 * hp + 2) * nq]], axis=0)
        o_ref[:, hp * 128:(hp + 1) * 128] = blk.T.astype(o_ref.dtype)


def _dsa(kk, avt, aqt, iqt, iwt, batch, seq, n_sel):
    t = batch * seq
    nqb = seq // DSA_QB
    qblk = lambda b, j: (0, b * nqb + j)
    return pl.pallas_call(
        functools.partial(_dsa_kernel, n_sel=n_sel, seq=seq),
        grid=(batch, nqb),
        in_specs=[
            pl.BlockSpec((seq, 128), lambda b, j: (b, 0)),
            pl.BlockSpec((AV_ROWS, seq), lambda b, j: (0, b)),
            pl.BlockSpec((A_WIDTH, DSA_QB), qblk),
            pl.BlockSpec((A_WIDTH, DSA_QB), qblk),
            pl.BlockSpec((IDX_HEADS, DSA_QB), qblk),
        ],
        out_specs=pl.BlockSpec((DSA_QB, A_WIDTH), lambda b, j: (b * nqb + j, 0)),
        out_shape=jax.ShapeDtypeStruct((t, A_WIDTH), BF16),
        scratch_shapes=[
            pltpu.VMEM((seq, A_HEADS * DSA_QB), F32),
            pltpu.VMEM((WORD, seq // WORD, DSA_QB), I32),
            pltpu.VMEM(((seq - 1).bit_length(), seq // WORD, DSA_QB), I32),
            pltpu.VMEM((seq // WORD, DSA_QB), I32),
            pltpu.VMEM((seq // WORD, DSA_QB), I32),
            pltpu.VMEM((128, A_HEADS * DSA_QB), BF16),
            pltpu.VMEM((128, A_HEADS * DSA_QB), BF16),
            pltpu.VMEM((AV_ROWS, A_HEADS * DSA_QB), F32),
            pltpu.VMEM((8, A_HEADS * DSA_QB), F32),
        ],
        compiler_params=pltpu.CompilerParams(
            dimension_semantics=("arbitrary", "arbitrary"), vmem_limit_bytes=VMEM_LIMIT),
        name="dsa",
    )(kk, avt, aqt, iqt, iwt)


def _log_gamma(h):
    return math.log1p(-(2.0 ** (-5.0 - h)))


def _retention_kernel(rq_ref, rkt_ref, rv_ref, rg_ref, gn_ref, o_ref,
                      state_ref, dmat_ref, dq_ref):
    nb = RET_BLOCK
    first = (pl.program_id(0) == 0) & (pl.program_id(1) == 0)

    @pl.when(first)
    def _():
        i = lax.broadcasted_iota(I32, (nb, nb), 0)
        jj = lax.broadcasted_iota(I32, (nb, nb), 1)
        dist = jnp.abs(i - jj).astype(F32)
        seen = (jj // CHUNK) <= (i // CHUNK)
        for h in range(R_HEADS):
            lg = _log_gamma(h)
            dmat_ref[h] = jnp.where(seen, jnp.exp(lg * dist), 0.0)
            dq_ref[h] = jnp.exp(lg * (i.astype(F32) + 1.0))

    @pl.when(pl.program_id(1) == 0)
    def _():
        state_ref[...] = jnp.zeros(state_ref.shape, F32)

    jrow = lax.broadcasted_iota(I32, (1, nb), 1).astype(F32)
    for h in range(R_HEADS):
        lg = _log_gamma(h)
        q = rq_ref[:, h * R_QK_DIM:(h + 1) * R_QK_DIM]
        kt = rkt_ref[h * R_QK_DIM:(h + 1) * R_QK_DIM, :]
        v = rv_ref[:, h * R_V_DIM:(h + 1) * R_V_DIM]
        state = state_ref[h]
        s = _dot(q, kt) * dmat_ref[h]
        y = _dot(s.astype(BF16), v) + _dot(q, state.astype(BF16)) * dq_ref[h]
        dk = jnp.exp(lg * (nb - 1.0 - jrow))
        kd = (kt.astype(F32) * dk).astype(BF16)
        state_ref[h] = state * math.exp(lg * nb) + _dot(kd, v)
        mean = jnp.mean(y, axis=-1, keepdims=True)
        yc = y - mean
        var = jnp.mean(yc * yc, axis=-1, keepdims=True)
        yn = yc * lax.rsqrt(var + GN_EPS) * gn_ref[:, h * R_V_DIM:(h + 1) * R_V_DIM]
        gate = rg_ref[:, h * R_V_DIM:(h + 1) * R_V_DIM].astype(F32)
        o_ref[:, h * R_V_DIM:(h + 1) * R_V_DIM] = (yn * (gate * jax.nn.sigmoid(gate))).astype(o_ref.dtype)


def _retention(rq, rkt, rv, rg, gn, batch, seq):
    t = batch * seq
    nblk = seq // RET_BLOCK
    tok = lambda b, n: (b * nblk + n, 0)
    return pl.pallas_call(
        _retention_kernel,
        grid=(batch, nblk),
        in_specs=[
            pl.BlockSpec((RET_BLOCK, 512), tok),
            pl.BlockSpec((512, RET_BLOCK), lambda b, n: (0, b * nblk + n)),
            pl.BlockSpec((RET_BLOCK, R_WIDTH), tok),
            pl.BlockSpec((RET_BLOCK, R_WIDTH), tok),
            pl.BlockSpec((1, R_WIDTH), lambda b, n: (0, 0)),
        ],
        out_specs=pl.BlockSpec((RET_BLOCK, R_WIDTH), tok),
        out_shape=jax.ShapeDtypeStruct((t, R_WIDTH), BF16),
        scratch_shapes=[
            pltpu.VMEM((R_HEADS, R_QK_DIM, R_V_DIM), F32),
            pltpu.VMEM((R_HEADS, RET_BLOCK, RET_BLOCK), F32),
            pltpu.VMEM((R_HEADS, RET_BLOCK, RET_BLOCK), F32),
        ],
        compiler_params=pltpu.CompilerParams(
            dimension_semantics=("arbitrary", "arbitrary"), vmem_limit_bytes=VMEM_LIMIT),
        name="retention",
    )(rq, rkt, rv, rg, gn)


def _tail_kernel(h_ref, ya_ref, yr_ref, ga_ref, gb_ref, p_ref,
                 wa_ref, wb_ref, wo_ref, g2_ref, wg_ref, wu_ref, wd_ref,
                 gp_ref, wpg_ref, wpp_ref, gf_ref, o_ref):
    a = _dot(ya_ref[...], wa_ref[...])
    b = _dot(yr_ref[...], wb_ref[...])
    merged = (jax.nn.sigmoid(ga_ref[...].astype(F32)) * a
              + jax.nn.sigmoid(gb_ref[...].astype(F32)) * b)
    h = h_ref[...] + _dot(merged.astype(BF16), wo_ref[...])
    h = h + 0.5 * _swiglu(_rms(h, g2_ref[...]).astype(BF16), wg_ref, wu_ref, wd_ref)
    gate = jax.nn.sigmoid(_dot(_rms(h, gp_ref[...]).astype(BF16), wpg_ref[...]))
    emb = _dot(p_ref[...].astype(BF16), wpp_ref[...])
    o_ref[...] = _rms(h + gate * emb, gf_ref[...])


def _tail(h, ya, yr, ga, gb, p, wa, wb, wo, g2, wg, wu, wd, gp, wpg, wpp, gf, tm):
    t = h.shape[0]
    row = lambda i: (i, 0)
    return pl.pallas_call(
        _tail_kernel,
        grid=(t // tm,),
        in_specs=[
            pl.BlockSpec((tm, D_MODEL), row),
            pl.BlockSpec((tm, A_WIDTH), row),
            pl.BlockSpec((tm, R_WIDTH), row),
            pl.BlockSpec((tm, D_MODEL), row),
            pl.BlockSpec((tm, D_MODEL), row),
            pl.BlockSpec((tm, PLE_DIM), row),
            _const_spec((A_WIDTH, D_MODEL)),
            _const_spec((R_WIDTH, D_MODEL)),
            _const_spec((D_MODEL, D_MODEL)),
            _const_spec((1, D_MODEL)),
            _const_spec((D_MODEL, D_FF)),
            _const_spec((D_MODEL, D_FF)),
            _const_spec((D_FF, D_MODEL)),
            _const_spec((1, D_MODEL)),
            _const_spec((D_MODEL, D_MODEL)),
            _const_spec((PLE_DIM, D_MODEL)),
            _const_spec((1, D_MODEL)),
        ],
        out_specs=pl.BlockSpec((tm, D_MODEL), row),
        out_shape=jax.ShapeDtypeStruct((t, D_MODEL), F32),
        compiler_params=pltpu.CompilerParams(
            dimension_semantics=("parallel",), vmem_limit_bytes=VMEM_LIMIT),
        name="tail",
    )(h, ya, yr, ga, gb, p, wa, wb, wo, g2, wg, wu, wd, gp, wpg, wpp, gf)


def _token_tile(t):
    return 512 if t % 512 == 0 else 256


def kernel(x, p, positions, ffn1_norm, ffn1_w_gate, ffn1_w_up, ffn1_w_down, mix_norm, w_in, ret_gn, w_branch_a, w_branch_b, w_out, ffn2_norm, ffn2_w_gate, ffn2_w_up, ffn2_w_down, ple_norm, w_ple_gate, w_ple_proj, final_norm):
    batch, seq, _ = x.shape
    depth = p.shape[0]
    t = batch * seq
    tm = _token_tile(t)
    n_sel = min(TOPK_MAX, seq // 4)
    assert depth == 1, "the final norm is fused into the per-layer embedding step"
    assert seq % RET_BLOCK == 0 and seq % KEY_TILE == 0 and seq % DSA_QB == 0
    assert n_sel <= KEY_TILE and DSA_QB == KEY_TILE and KEY_TILE == 8 * WORD

    h = x.reshape(t, D_MODEL)
    pos3 = positions.reshape(t // tm, 1, tm)
    vec = lambda g: g.reshape(1, -1).astype(F32)

    for i in range(depth):
        w = w_in[i]
        wt = jnp.concatenate([
            w[:, _OFF_AQ:_OFF_AQ + A_WIDTH], w[:, _OFF_IQ:_OFF_IQ + IDX_HEADS * IDX_DIM],
            w[:, _OFF_RQ:_OFF_RQ + 512], w[:, _OFF_RK:_OFF_RK + 512],
            w[:, _OFF_AK:_OFF_AK + 64], w[:, _OFF_IK:_OFF_IK + 64],
            w[:, _OFF_AV:_OFF_AV + 64], w[:, _OFF_IW:_OFF_IW + IDX_HEADS]], axis=1).T.astype(BF16)
        wn = w[:, _OFF_RV:].astype(BF16)

        h = _ffn(h, vec(ffn1_norm[i]), ffn1_w_gate[i].astype(BF16), ffn1_w_up[i].astype(BF16),
                 ffn1_w_down[i].astype(BF16), tm)
        aqt, iqt, rq, rkt, kk, avt, iwt, rv, rg, ga, gb = _inproj(h, vec(mix_norm[i]), pos3, wt, wn, tm)
        ya = _dsa(kk, avt, aqt, iqt, iwt, batch, seq, n_sel)
        yr = _retention(rq, rkt, rv, rg, vec(ret_gn[i]), batch, seq)
        h = _tail(h, ya, yr, ga, gb, p[i].reshape(t, PLE_DIM),
                  w_branch_a[i].astype(BF16), w_branch_b[i].astype(BF16), w_out[i].astype(BF16),
                  vec(ffn2_norm[i]), ffn2_w_gate[i].astype(BF16), ffn2_w_up[i].astype(BF16),
                  ffn2_w_down[i].astype(BF16),
                  vec(ple_norm[i]), w_ple_gate[i].astype(BF16), w_ple_proj[i].astype(BF16),
                  vec(final_norm), tm)
    return h.reshape(batch, seq, D_MODEL)
```

```python
import functools
import math

import jax
import jax.numpy as jnp
import numpy as np
from jax import lax
from jax.experimental import pallas as pl
from jax.experimental.pallas import tpu as pltpu

F32 = jnp.float32
BF16 = jnp.bfloat16
I32 = jnp.int32

D_MODEL = 1024
CHUNK = 64
ROPE_THETA = 10000.0
EPS = 1e-6
GN_EPS = 1e-5
NEG_BIG = -1e30
A_HEADS = 8
A_HEAD_DIM = 64
IDX_HEADS = 8
IDX_DIM = 64
TOPK_MAX = 256
A_WIDTH = A_HEADS * A_HEAD_DIM
R_HEADS = 4
R_QK_DIM = 128
R_V_DIM = 256
R_WIDTH = R_HEADS * R_V_DIM
D_FF = 2816
PLE_DIM = 256

_OFF_AQ = 0
_OFF_AK = _OFF_AQ + A_WIDTH
_OFF_AV = _OFF_AK + A_HEAD_DIM
_OFF_IQ = _OFF_AV + A_HEAD_DIM
_OFF_IK = _OFF_IQ + IDX_HEADS * IDX_DIM
_OFF_IW = _OFF_IK + IDX_DIM
_OFF_RQ = _OFF_IW + IDX_HEADS
_OFF_RK = _OFF_RQ + R_HEADS * R_QK_DIM
_OFF_RV = _OFF_RK + R_HEADS * R_QK_DIM
_N_TOKEN_MAJOR = 2 * R_WIDTH + 2 * D_MODEL

_T_AQ = 0
_T_IQ = _T_AQ + A_WIDTH
_T_RQ = _T_IQ + IDX_HEADS * IDX_DIM
_T_RK = _T_RQ + R_HEADS * R_QK_DIM
_T_AK = _T_RK + R_HEADS * R_QK_DIM
_T_IK = _T_AK + A_HEAD_DIM
_T_AV = _T_IK + IDX_DIM
_T_IW = _T_AV + A_HEAD_DIM
_T_ROWS = _T_IW + IDX_HEADS

VMEM_LIMIT = 56 * 1024 * 1024
FF_CHUNK = 256
DSA_QB = 256
KEY_TILE = 256
KEY_HALF = KEY_TILE // 2
RET_BLOCK = 256

AV_ROWS = A_HEAD_DIM + 16
INT_MIN = -(2 ** 31)
WORD = 32
LOG2_E = math.log2(math.e)
DENOM_MIN = 2.0 ** -100


def _const_spec(shape):
    nd = len(shape)
    return pl.BlockSpec(shape, lambda *_: (0,) * nd, pipeline_mode=pl.Buffered(1))


def _rms(x, g):
    ms = jnp.mean(x * x, axis=-1, keepdims=True)
    return x * lax.rsqrt(ms + EPS) * g


def _dot(a, b):
    return jnp.dot(a, b, preferred_element_type=F32)


def _swiglu(u, wg_ref, wu_ref, wd_ref):
    acc = jnp.zeros((u.shape[0], D_MODEL), F32)
    for c in range(D_FF // FF_CHUNK):
        sl = slice(c * FF_CHUNK, (c + 1) * FF_CHUNK)
        gate = _dot(u, wg_ref[:, sl])
        up = _dot(u, wu_ref[:, sl])
        act = (gate * jax.nn.sigmoid(gate) * up).astype(BF16)
        acc = acc + _dot(act, wd_ref[sl, :])
    return acc


def _ffn_kernel(h_ref, g_ref, wg_ref, wu_ref, wd_ref, o_ref):
    h = h_ref[...]
    o_ref[...] = h + 0.5 * _swiglu(_rms(h, g_ref[...]).astype(BF16), wg_ref, wu_ref, wd_ref)


def _ffn(h, g, wg, wu, wd, tm):
    t = h.shape[0]
    return pl.pallas_call(
        _ffn_kernel,
        grid=(t // tm,),
        in_specs=[
            pl.BlockSpec((tm, D_MODEL), lambda i: (i, 0)),
            _const_spec((1, D_MODEL)),
            _const_spec((D_MODEL, D_FF)),
            _const_spec((D_MODEL, D_FF)),
            _const_spec((D_FF, D_MODEL)),
        ],
        out_specs=pl.BlockSpec((tm, D_MODEL), lambda i: (i, 0)),
        out_shape=jax.ShapeDtypeStruct((t, D_MODEL), F32),
        compiler_params=pltpu.CompilerParams(
            dimension_semantics=("parallel",), vmem_limit_bytes=VMEM_LIMIT),
        name="ffn",
    )(h, g, wg, wu, wd)


def _rope_rows(t1, t2, cos, sin):
    return t1 * cos - t2 * sin, t2 * cos + t1 * sin


def _inproj_kernel(h_ref, g_ref, pos_ref, wt_ref, wn_ref,
                   aqt_ref, iqt_ref, rq_ref, rkt_ref, kk_ref, avt_ref, iwt_ref,
                   rv_ref, rg_ref, ga_ref, gb_ref):
    tm = h_ref.shape[0]
    u = _rms(h_ref[...], g_ref[...]).astype(BF16)

    zn = _dot(u, wn_ref[...])
    rv_ref[...] = zn[:, 0:R_WIDTH].astype(BF16)
    rg_ref[...] = zn[:, R_WIDTH:2 * R_WIDTH].astype(BF16)
    ga_ref[...] = zn[:, 2 * R_WIDTH:2 * R_WIDTH + D_MODEL].astype(BF16)
    gb_ref[...] = zn[:, 2 * R_WIDTH + D_MODEL:].astype(BF16)

    zt = lax.dot_general(wt_ref[...], u, (((1,), (1,)), ((), ())),
                         preferred_element_type=F32)

    pos = pos_ref[0].astype(F32)

    def angles(half):
        i = lax.broadcasted_iota(I32, (half, tm), 0).astype(F32)
        inv = jnp.exp(i * (-math.log(ROPE_THETA) / half))
        ang = pos * inv
        return jnp.cos(ang), jnp.sin(ang)

    cos64, sin64 = angles(A_HEAD_DIM // 2)
    cos128, sin128 = angles(R_QK_DIM // 2)

    def rope_head(row0, dim, cos, sin):
        half = dim // 2
        return _rope_rows(zt[row0:row0 + half], zt[row0 + half:row0 + dim], cos, sin)

    for h in range(A_HEADS):
        o1, o2 = rope_head(_T_AQ + h * A_HEAD_DIM, A_HEAD_DIM, cos64, sin64)
        r = h * A_HEAD_DIM
        aqt_ref[r:r + 32, :] = (o1 * (A_HEAD_DIM ** -0.5 * LOG2_E)).astype(BF16)
        aqt_ref[r + 32:r + 64, :] = (o2 * (A_HEAD_DIM ** -0.5 * LOG2_E)).astype(BF16)
    for h in range(IDX_HEADS):
        o1, o2 = rope_head(_T_IQ + h * IDX_DIM, IDX_DIM, cos64, sin64)
        r = h * IDX_DIM
        iqt_ref[r:r + 32, :] = (o1 * (IDX_DIM ** -0.5)).astype(BF16)
        iqt_ref[r + 32:r + 64, :] = (o2 * (IDX_DIM ** -0.5)).astype(BF16)

    for h in range(R_HEADS):
        o1, o2 = rope_head(_T_RQ + h * R_QK_DIM, R_QK_DIM, cos128, sin128)
        q = jnp.concatenate([o1, o2], axis=0)
        rq_ref[:, h * R_QK_DIM:(h + 1) * R_QK_DIM] = q.T.astype(BF16)
        k1, k2 = rope_head(_T_RK + h * R_QK_DIM, R_QK_DIM, cos128, sin128)
        r = h * R_QK_DIM
        rkt_ref[r:r + 64, :] = (k1 * (R_QK_DIM ** -0.5)).astype(BF16)
        rkt_ref[r + 64:r + 128, :] = (k2 * (R_QK_DIM ** -0.5)).astype(BF16)

    a1, a2 = rope_head(_T_AK, A_HEAD_DIM, cos64, sin64)
    i1, i2 = rope_head(_T_IK, IDX_DIM, cos64, sin64)
    kk = jnp.concatenate([a1, a2, i1, i2], axis=0)
    kk_ref[...] = kk.T.astype(BF16)

    avt_ref[0:A_HEAD_DIM, :] = zt[_T_AV:_T_AV + A_HEAD_DIM].astype(BF16)
    avt_ref[A_HEAD_DIM:AV_ROWS, :] = jnp.ones((AV_ROWS - A_HEAD_DIM, tm), BF16)
    iwt_ref[...] = zt[_T_IW:_T_IW + IDX_HEADS] * (IDX_HEADS ** -0.5)


def _inproj(h, g, pos3, wt, wn, tm):
    t = h.shape[0]
    nt = t // tm
    row = lambda i: (i, 0)
    col = lambda i: (0, i)
    outs = [
        (jax.ShapeDtypeStruct((A_WIDTH, t), BF16), pl.BlockSpec((A_WIDTH, tm), col)),
        (jax.ShapeDtypeStruct((A_WIDTH, t), BF16), pl.BlockSpec((A_WIDTH, tm), col)),
        (jax.ShapeDtypeStruct((t, 512), BF16), pl.BlockSpec((tm, 512), row)),
        (jax.ShapeDtypeStruct((512, t), BF16), pl.BlockSpec((512, tm), col)),
        (jax.ShapeDtypeStruct((t, 128), BF16), pl.BlockSpec((tm, 128), row)),
        (jax.ShapeDtypeStruct((AV_ROWS, t), BF16), pl.BlockSpec((AV_ROWS, tm), col)),
        (jax.ShapeDtypeStruct((IDX_HEADS, t), F32), pl.BlockSpec((IDX_HEADS, tm), col)),
        (jax.ShapeDtypeStruct((t, R_WIDTH), BF16), pl.BlockSpec((tm, R_WIDTH), row)),
        (jax.ShapeDtypeStruct((t, R_WIDTH), BF16), pl.BlockSpec((tm, R_WIDTH), row)),
        (jax.ShapeDtypeStruct((t, D_MODEL), BF16), pl.BlockSpec((tm, D_MODEL), row)),
        (jax.ShapeDtypeStruct((t, D_MODEL), BF16), pl.BlockSpec((tm, D_MODEL), row)),
    ]
    return pl.pallas_call(
        _inproj_kernel,
        grid=(nt,),
        in_specs=[
            pl.BlockSpec((tm, D_MODEL), row),
            _const_spec((1, D_MODEL)),
            pl.BlockSpec((1, 1, tm), lambda i: (i, 0, 0)),
            _const_spec((_T_ROWS, D_MODEL)),
            _const_spec((D_MODEL, _N_TOKEN_MAJOR)),
        ],
        out_specs=[o[1] for o in outs],
        out_shape=[o[0] for o in outs],
        compiler_params=pltpu.CompilerParams(
            dimension_semantics=("parallel",), vmem_limit_bytes=VMEM_LIMIT),
        name="inproj",
    )(h, g, pos3, wt, wn)


def _tree_or(parts):
    while len(parts) > 1:
        parts = [parts[i] | parts[i + 1] for i in range(0, len(parts) - 1, 2)] + (
            [parts[-1]] if len(parts) % 2 else [])
    return parts[0]


_BT_MASKS = {16: 0x0000FFFF, 8: 0x00FF00FF, 4: 0x0F0F0F0F, 2: 0x33333333, 1: 0x55555555}


def _bit_transpose32(rows):
    a = list(rows)
    j = 16
    while j:
        mask = _BT_MASKS[j]
        k = 0
        while k < WORD:
            t = (a[k] ^ (a[k + j] >> j)) & mask
            a[k] = a[k] ^ t
            a[k + j] = a[k + j] ^ (t << j)
            k = (k + j + 1) & ~j
        j >>= 1
    return a


def _word_bit(m):
    v = 1 << (WORD - 1 - m)
    return v - (1 << WORD) if v >= (1 << (WORD - 1)) else v


def _dsa_kernel(kk_ref, avt_ref, aqt_ref, iqt_ref, iwt_ref, o_ref,
                lg_ref, planes_ref, posp_ref, valid_ref, sel_ref, qa_ref, qi_ref,
                acc_ref, ml_ref, *, n_sel, seq):
    j = pl.program_id(1)
    nq = DSA_QB
    n_tiles = j + 1
    words = seq // WORD
    wpt = KEY_TILE // WORD
    pos_bits = (seq - 1).bit_length()

    @pl.when((pl.program_id(0) == 0) & (j == 0))
    def _():
        qa_ref[64:128, :] = jnp.zeros((64, A_HEADS * nq), BF16)
        qi_ref[0:64, :] = jnp.zeros((64, A_HEADS * nq), BF16)
        planes_ref[...] = jnp.zeros(planes_ref.shape, I32)
        valid_ref[...] = jnp.zeros(valid_ref.shape, I32)
        for t in range(seq // KEY_TILE):
            sub = lax.broadcasted_iota(I32, (wpt, nq), 0)
            p = _bit_transpose32([(seq - 1 - t * KEY_TILE - m * wpt) - sub for m in range(WORD)])
            for i in range(pos_bits):
                posp_ref[i, t * wpt:(t + 1) * wpt, :] = p[WORD - pos_bits + i]

    for h in range(A_HEADS):
        cs = slice(h * nq, (h + 1) * nq)
        qa_ref[0:64, cs] = aqt_ref[h * 64:(h + 1) * 64, :]
        qi_ref[64:128, cs] = iqt_ref[h * 64:(h + 1) * 64, :]

    lane = lax.broadcasted_iota(I32, (1, nq), 1)
    vis_end = (j * nq + (lane // CHUNK + 1) * CHUNK)
    w = iwt_ref[...]

    def tile_rows(t):
        return pl.ds(pl.multiple_of(t * KEY_TILE, KEY_TILE), KEY_TILE)

    def word_rows(t):
        return pl.ds(pl.multiple_of(t * wpt, wpt), wpt)

    def for_each_tile(body):
        def pair(i, carry):
            body(2 * i)
            body(2 * i + 1)
            return carry

        lax.fori_loop(0, n_tiles // 2, pair, 0)

        @pl.when(n_tiles % 2 == 1)
        def _():
            body(n_tiles - 1)

    ml_ref[1:2, :] = jnp.full((1, A_HEADS * nq), 0.5 * NEG_BIG, F32)

    def score_tile(t):
        keys, valid = [], []
        for half in range(2):
            r0 = pl.multiple_of(t * KEY_TILE + half * KEY_HALF, KEY_HALF)
            kt = kk_ref[pl.ds(r0, KEY_HALF), :]
            sc = None
            for h in range(IDX_HEADS):
                rel = jnp.maximum(_dot(kt, qi_ref[:, h * nq:(h + 1) * nq]), 0.0)
                part = rel * w[h:h + 1, :]
                sc = part if sc is None else sc + part
            sc = jnp.where(sc == 0.0, 0.0, sc)
            spos = r0 + lax.broadcasted_iota(I32, (KEY_HALF, nq), 0)
            sc = jnp.where(spos < vis_end, sc, NEG_BIG)
            bits = pltpu.bitcast(sc, I32)
            keys.append(bits ^ ((bits >> 31) | jnp.int32(INT_MIN)))
            valid.append(sc > 0.5 * NEG_BIG)

        kt_all = kk_ref[tile_rows(t), :]
        for h in range(A_HEADS):
            cs = slice(h * nq, (h + 1) * nq)
            lg = _dot(kt_all, qa_ref[:, cs])
            lg_ref[tile_rows(t), cs] = lg
            ml_ref[1:2, cs] = jnp.maximum(ml_ref[1:2, cs], jnp.max(lg, axis=0, keepdims=True))

        def member(parts, m):
            r = (m % 16) * 8
            return parts[m // 16][r:r + 8, :]

        planes = _bit_transpose32([member(keys, m) for m in range(WORD)])
        for i in range(WORD):
            planes_ref[i, word_rows(t), :] = planes[i]
        valid_ref[word_rows(t), :] = _tree_or(
            [jnp.where(member(valid, m), jnp.int32(_word_bit(m)), jnp.int32(0)) for m in range(WORD)])

    for_each_tile(score_tile)

    wrow = lax.broadcasted_iota(I32, (words, nq), 0)
    state = (jnp.where(wrow < n_tiles * wpt, jnp.int32(-1), jnp.int32(0)),
             jnp.zeros((words, nq), I32),
             jnp.zeros((1, nq), I32))

    def radix_step(plane, state):
        alive, sel, above = state
        ones = alive & plane
        c = jnp.sum(lax.population_count(ones), axis=0, keepdims=True)
        keep_ones = (above + c) >= n_sel
        return (jnp.where(keep_ones, ones, alive ^ ones),
                jnp.where(keep_ones, sel, sel | ones),
                jnp.where(keep_ones, above, above + c))

    state = lax.fori_loop(0, WORD, lambda i, s: radix_step(planes_ref[i], s), state)
    state = lax.fori_loop(0, pos_bits, lambda i, s: radix_step(posp_ref[i], s), state)
    alive, sel, _ = state
    sel_ref[...] = (sel | alive) & valid_ref[...]

    def masked_logits(t, cs, bias):
        return lg_ref[tile_rows(t), cs] + bias

    def mask_bias(t):
        selw = sel_ref[word_rows(t), :]
        return jnp.concatenate(
            [jnp.where((selw << m) < 0, 0.0, NEG_BIG) for m in range(WORD)], axis=0)

    def attend(shift_row):
        acc_ref[...] = jnp.zeros(acc_ref.shape, F32)

        def attn_tile(t):
            vt = avt_ref[:, tile_rows(t)]
            bias = mask_bias(t)
            for h in range(A_HEADS):
                cs = slice(h * nq, (h + 1) * nq)
                p = jnp.exp2(masked_logits(t, cs, bias) - ml_ref[shift_row:shift_row + 1, cs])
                acc_ref[:, cs] = acc_ref[:, cs] + _dot(vt, p.astype(BF16))

        for_each_tile(attn_tile)

    attend(1)
    underflow = jnp.min(acc_ref[A_HEAD_DIM:A_HEAD_DIM + 1, :]) < DENOM_MIN

    @pl.when(underflow)
    def _():
        ml_ref[0:1, :] = jnp.full((1, A_HEADS * nq), 0.5 * NEG_BIG, F32)

        def max_tile(t, carry):
            bias = mask_bias(t)
            for h in range(A_HEADS):
                cs = slice(h * nq, (h + 1) * nq)
                ml_ref[0:1, cs] = jnp.maximum(
                    ml_ref[0:1, cs], jnp.max(masked_logits(t, cs, bias), axis=0, keepdims=True))
            return carry

        lax.fori_loop(0, n_tiles, max_tile, 0)
        attend(0)

    out_t = acc_ref[0:A_HEAD_DIM, :] / acc_ref[A_HEAD_DIM:A_HEAD_DIM + 1, :]
    for hp in range(A_HEADS // 2):
        blk = jnp.concatenate(
            [out_t[:, (2 * hp) * nq:(2 * hp + 1) * nq], out_t[:, (2 * hp + 1) * nq:(2 * hp + 2) * nq]], axis=0)
        o_ref[:, hp * 128:(hp + 1) * 128] = blk.T.astype(o_ref.dtype)


def _dsa(kk, avt, aqt, iqt, iwt, batch, seq, n_sel):
    t = batch * seq
    nqb = seq // DSA_QB
    qblk = lambda b, j: (0, b * nqb + j)
    return pl.pallas_call(
        functools.partial(_dsa_kernel, n_sel=n_sel, seq=seq),
        grid=(batch, nqb),
        in_specs=[
            pl.BlockSpec((seq, 128), lambda b, j: (b, 0)),
            pl.BlockSpec((AV_ROWS, seq), lambda b, j: (0, b)),
            pl.BlockSpec((A_WIDTH, DSA_QB), qblk),
            pl.BlockSpec((A_WIDTH, DSA_QB), qblk),
            pl.BlockSpec((IDX_HEADS, DSA_QB), qblk),
        ],
        out_specs=pl.BlockSpec((DSA_QB, A_WIDTH), lambda b, j: (b * nqb + j, 0)),
        out_shape=jax.ShapeDtypeStruct((t, A_WIDTH), BF16),
        scratch_shapes=[
            pltpu.VMEM((seq, A_HEADS * DSA_QB), F32),
            pltpu.VMEM((WORD, seq // WORD, DSA_QB), I32),
            pltpu.VMEM(((seq - 1).bit_length(), seq // WORD, DSA_QB), I32),
            pltpu.VMEM((seq // WORD, DSA_QB), I32),
            pltpu.VMEM((seq // WORD, DSA_QB), I32),
            pltpu.VMEM((128, A_HEADS * DSA_QB), BF16),
            pltpu.VMEM((128, A_HEADS * DSA_QB), BF16),
            pltpu.VMEM((AV_ROWS, A_HEADS * DSA_QB), F32),
            pltpu.VMEM((8, A_HEADS * DSA_QB), F32),
        ],
        compiler_params=pltpu.CompilerParams(
            dimension_semantics=("arbitrary", "arbitrary"), vmem_limit_bytes=VMEM_LIMIT),
        name="dsa",
    )(kk, avt, aqt, iqt, iwt)


def _log_gamma(h):
    return math.log1p(-(2.0 ** (-5.0 - h)))


def _retention_kernel(rq_ref, rkt_ref, rv_ref, rg_ref, gn_ref, o_ref,
                      state_ref, dmat_ref, dq_ref):
    nb = RET_BLOCK
    first = (pl.program_id(0) == 0) & (pl.program_id(1) == 0)

    @pl.when(first)
    def _():
        i = lax.broadcasted_iota(I32, (nb, nb), 0)
        jj = lax.broadcasted_iota(I32, (nb, nb), 1)
        dist = jnp.abs(i - jj).astype(F32)
        seen = (jj // CHUNK) <= (i // CHUNK)
        for h in range(R_HEADS):
            lg = _log_gamma(h)
            dmat_ref[h] = jnp.where(seen, jnp.exp(lg * dist), 0.0)
            dq_ref[h] = jnp.exp(lg * (i.astype(F32) + 1.0))

    @pl.when(pl.program_id(1) == 0)
    def _():
        state_ref[...] = jnp.zeros(state_ref.shape, F32)

    jrow = lax.broadcasted_iota(I32, (1, nb), 1).astype(F32)
    for h in range(R_HEADS):
        lg = _log_gamma(h)
        q = rq_ref[:, h * R_QK_DIM:(h + 1) * R_QK_DIM]
        kt = rkt_ref[h * R_QK_DIM:(h + 1) * R_QK_DIM, :]
        v = rv_ref[:, h * R_V_DIM:(h + 1) * R_V_DIM]
        state = state_ref[h]
        s = _dot(q, kt) * dmat_ref[h]
        y = _dot(s.astype(BF16), v) + _dot(q, state.astype(BF16)) * dq_ref[h]
        dk = jnp.exp(lg * (nb - 1.0 - jrow))
        kd = (kt.astype(F32) * dk).astype(BF16)
        state_ref[h] = state * math.exp(lg * nb) + _dot(kd, v)
        mean = jnp.mean(y, axis=-1, keepdims=True)
        yc = y - mean
        var = jnp.mean(yc * yc, axis=-1, keepdims=True)
        yn = yc * lax.rsqrt(var + GN_EPS) * gn_ref[:, h * R_V_DIM:(h + 1) * R_V_DIM]
        gate = rg_ref[:, h * R_V_DIM:(h + 1) * R_V_DIM].astype(F32)
        o_ref[:, h * R_V_DIM:(h + 1) * R_V_DIM] = (yn * (gate * jax.nn.sigmoid(gate))).astype(o_ref.dtype)


def _retention(rq, rkt, rv, rg, gn, batch, seq):
    t = batch * seq
    nblk = seq // RET_BLOCK
    tok = lambda b, n: (b * nblk + n, 0)
    return pl.pallas_call(
        _retention_kernel,
        grid=(batch, nblk),
        in_specs=[
            pl.BlockSpec((RET_BLOCK, 512), tok),
            pl.BlockSpec((512, RET_BLOCK), lambda b, n: (0, b * nblk + n)),
            pl.BlockSpec((RET_BLOCK, R_WIDTH), tok),
            pl.BlockSpec((RET_BLOCK, R_WIDTH), tok),
            pl.BlockSpec((1, R_WIDTH), lambda b, n: (0, 0)),
        ],
        out_specs=pl.BlockSpec((RET_BLOCK, R_WIDTH), tok),
        out_shape=jax.ShapeDtypeStruct((t, R_WIDTH), BF16),
        scratch_shapes=[
            pltpu.VMEM((R_HEADS, R_QK_DIM, R_V_DIM), F32),
            pltpu.VMEM((R_HEADS, RET_BLOCK, RET_BLOCK), F32),
            pltpu.VMEM((R_HEADS, RET_BLOCK, RET_BLOCK), F32),
        ],
        compiler_params=pltpu.CompilerParams(
            dimension_semantics=("arbitrary", "arbitrary"), vmem_limit_bytes=VMEM_LIMIT),
        name="retention",
    )(rq, rkt, rv, rg, gn)


def _tail_kernel(h_ref, ya_ref, yr_ref, ga_ref, gb_ref, p_ref,
                 wa_ref, wb_ref, wo_ref, g2_ref, wg_ref, wu_ref, wd_ref,
                 gp_ref, wpg_ref, wpp_ref, gf_ref, o_ref):
    a = _dot(ya_ref[...], wa_ref[...])
    b = _dot(yr_ref[...], wb_ref[...])
    merged = (jax.nn.sigmoid(ga_ref[...].astype(F32)) * a
              + jax.nn.sigmoid(gb_ref[...].astype(F32)) * b)
    h = h_ref[...] + _dot(merged.astype(BF16), wo_ref[...])
    h = h + 0.5 * _swiglu(_rms(h, g2_ref[...]).astype(BF16), wg_ref, wu_ref, wd_ref)
    gate = jax.nn.sigmoid(_dot(_rms(h, gp_ref[...]).astype(BF16), wpg_ref[...]))
    emb = _dot(p_ref[...].astype(BF16), wpp_ref[...])
    o_ref[...] = _rms(h + gate * emb, gf_ref[...])


def _tail(h, ya, yr, ga, gb, p, wa, wb, wo, g2, wg, wu, wd, gp, wpg, wpp, gf, tm):
    t = h.shape[0]
    row = lambda i: (i, 0)
    return pl.pallas_call(
        _tail_kernel,
        grid=(t // tm,),
        in_specs=[
            pl.BlockSpec((tm, D_MODEL), row),
            pl.BlockSpec((tm, A_WIDTH), row),
            pl.BlockSpec((tm, R_WIDTH), row),
            pl.BlockSpec((tm, D_MODEL), row),
            pl.BlockSpec((tm, D_MODEL), row),
            pl.BlockSpec((tm, PLE_DIM), row),
            _const_spec((A_WIDTH, D_MODEL)),
            _const_spec((R_WIDTH, D_MODEL)),
            _const_spec((D_MODEL, D_MODEL)),
            _const_spec((1, D_MODEL)),
            _const_spec((D_MODEL, D_FF)),
            _const_spec((D_MODEL, D_FF)),
            _const_spec((D_FF, D_MODEL)),
            _const_spec((1, D_MODEL)),
            _const_spec((D_MODEL, D_MODEL)),
            _const_spec((PLE_DIM, D_MODEL)),
            _const_spec((1, D_MODEL)),
        ],
        out_specs=pl.BlockSpec((tm, D_MODEL), row),
        out_shape=jax.ShapeDtypeStruct((t, D_MODEL), F32),
        compiler_params=pltpu.CompilerParams(
            dimension_semantics=("parallel",), vmem_limit_bytes=VMEM_LIMIT),
        name="tail",
    )(h, ya, yr, ga, gb, p, wa, wb, wo, g2, wg, wu, wd, gp, wpg, wpp, gf)


def _token_tile(t):
    return 512 if t % 512 == 0 else 256


def kernel(x, p, positions, ffn1_norm, ffn1_w_gate, ffn1_w_up, ffn1_w_down, mix_norm, w_in, ret_gn, w_branch_a, w_branch_b, w_out, ffn2_norm, ffn2_w_gate, ffn2_w_up, ffn2_w_down, ple_norm, w_ple_gate, w_ple_proj, final_norm):
    batch, seq, _ = x.shape
    depth = p.shape[0]
    t = batch * seq
    tm = _token_tile(t)
    n_sel = min(TOPK_MAX, seq // 4)
    assert depth == 1, "the final norm is fused into the per-layer embedding step"
    assert seq % RET_BLOCK == 0 and seq % KEY_TILE == 0 and seq % DSA_QB == 0
    assert n_sel <= KEY_TILE and DSA_QB == KEY_TILE and KEY_TILE == 8 * WORD

    h = x.reshape(t, D_MODEL)
    pos3 = positions.reshape(t // tm, 1, tm)
    vec = lambda g: g.reshape(1, -1).astype(F32)

    for i in range(depth):
        w = w_in[i].astype(BF16)
        wt = jnp.concatenate([
            w[:, _OFF_AQ:_OFF_AQ + A_WIDTH], w[:, _OFF_IQ:_OFF_IQ + IDX_HEADS * IDX_DIM],
            w[:, _OFF_RQ:_OFF_RQ + 512], w[:, _OFF_RK:_OFF_RK + 512],
            w[:, _OFF_AK:_OFF_AK + 64], w[:, _OFF_IK:_OFF_IK + 64],
            w[:, _OFF_AV:_OFF_AV + 64], w[:, _OFF_IW:_OFF_IW + IDX_HEADS]], axis=1).T
        wn = w[:, _OFF_RV:]

        h = _ffn(h, vec(ffn1_norm[i]), ffn1_w_gate[i].astype(BF16), ffn1_w_up[i].astype(BF16),
                 ffn1_w_down[i].astype(BF16), tm)
        aqt, iqt, rq, rkt, kk, avt, iwt, rv, rg, ga, gb = _inproj(h, vec(mix_norm[i]), pos3, wt, wn, tm)
        ya = _dsa(kk, avt, aqt, iqt, iwt, batch, seq, n_sel)
        yr = _retention(rq, rkt, rv, rg, vec(ret_gn[i]), batch, seq)
        h = _tail(h, ya, yr, ga, gb, p[i].reshape(t, PLE_DIM),
                  w_branch_a[i].astype(BF16), w_branch_b[i].astype(BF16), w_out[i].astype(BF16),
                  vec(ffn2_norm[i]), ffn2_w_gate[i].astype(BF16), ffn2_w_up[i].astype(BF16),
                  ffn2_w_down[i].astype(BF16),
                  vec(ple_norm[i]), w_ple_gate[i].astype(BF16), w_ple_proj[i].astype(BF16),
                  vec(final_norm), tm)
    return h.reshape(batch, seq, D_MODEL)
```

```python
import functools
import math

import jax
import jax.numpy as jnp
import numpy as np
from jax import lax
from jax.experimental import pallas as pl
from jax.experimental.pallas import tpu as pltpu

F32 = jnp.float32
BF16 = jnp.bfloat16
I32 = jnp.int32

D_MODEL = 1024
CHUNK = 64
ROPE_THETA = 10000.0
EPS = 1e-6
GN_EPS = 1e-5
NEG_BIG = -1e30
A_HEADS = 8
A_HEAD_DIM = 64
IDX_HEADS = 8
IDX_DIM = 64
TOPK_MAX = 256
A_WIDTH = A_HEADS * A_HEAD_DIM
R_HEADS = 4
R_QK_DIM = 128
R_V_DIM = 256
R_WIDTH = R_HEADS * R_V_DIM
D_FF = 2816
PLE_DIM = 256

_OFF_AQ = 0
_OFF_AK = _OFF_AQ + A_WIDTH
_OFF_AV = _OFF_AK + A_HEAD_DIM
_OFF_IQ = _OFF_AV + A_HEAD_DIM
_OFF_IK = _OFF_IQ + IDX_HEADS * IDX_DIM
_OFF_IW = _OFF_IK + IDX_DIM
_OFF_RQ = _OFF_IW + IDX_HEADS
_OFF_RK = _OFF_RQ + R_HEADS * R_QK_DIM
_OFF_RV = _OFF_RK + R_HEADS * R_QK_DIM
_N_TOKEN_MAJOR = 2 * R_WIDTH + 2 * D_MODEL

_T_AQ = 0
_T_IQ = _T_AQ + A_WIDTH
_T_RQ = _T_IQ + IDX_HEADS * IDX_DIM
_T_RK = _T_RQ + R_HEADS * R_QK_DIM
_T_AK = _T_RK + R_HEADS * R_QK_DIM
_T_IK = _T_AK + A_HEAD_DIM
_T_AV = _T_IK + IDX_DIM
_T_IW = _T_AV + A_HEAD_DIM
_T_ROWS = _T_IW + IDX_HEADS

VMEM_LIMIT = 56 * 1024 * 1024
FF_CHUNK = 256
DSA_QB = 256
KEY_TILE = 256
KEY_HALF = KEY_TILE // 2
TILE_UNROLL = 4
RET_BLOCK = 256

AV_ROWS = A_HEAD_DIM + 16
INT_MIN = -(2 ** 31)
WORD = 32
LOG2_E = math.log2(math.e)
DENOM_MIN = 2.0 ** -100


def _const_spec(shape):
    nd = len(shape)
    return pl.BlockSpec(shape, lambda *_: (0,) * nd, pipeline_mode=pl.Buffered(1))


def _rms(x, g):
    ms = jnp.mean(x * x, axis=-1, keepdims=True)
    return x * lax.rsqrt(ms + EPS) * g


def _dot(a, b):
    return jnp.dot(a, b, preferred_element_type=F32)


def _swiglu(u, wg_ref, wu_ref, wd_ref):
    acc = jnp.zeros((u.shape[0], D_MODEL), F32)
    for c in range(D_FF // FF_CHUNK):
        sl = slice(c * FF_CHUNK, (c + 1) * FF_CHUNK)
        gate = _dot(u, wg_ref[:, sl])
        up = _dot(u, wu_ref[:, sl])
        act = (gate * jax.nn.sigmoid(gate) * up).astype(BF16)
        acc = acc + _dot(act, wd_ref[sl, :])
    return acc


def _ffn_kernel(h_ref, g_ref, wg_ref, wu_ref, wd_ref, o_ref):
    h = h_ref[...]
    o_ref[...] = h + 0.5 * _swiglu(_rms(h, g_ref[...]).astype(BF16), wg_ref, wu_ref, wd_ref)


def _ffn(h, g, wg, wu, wd, tm):
    t = h.shape[0]
    return pl.pallas_call(
        _ffn_kernel,
        grid=(t // tm,),
        in_specs=[
            pl.BlockSpec((tm, D_MODEL), lambda i: (i, 0)),
            _const_spec((1, D_MODEL)),
            _const_spec((D_MODEL, D_FF)),
            _const_spec((D_MODEL, D_FF)),
            _const_spec((D_FF, D_MODEL)),
        ],
        out_specs=pl.BlockSpec((tm, D_MODEL), lambda i: (i, 0)),
        out_shape=jax.ShapeDtypeStruct((t, D_MODEL), F32),
        compiler_params=pltpu.CompilerParams(
            dimension_semantics=("parallel",), vmem_limit_bytes=VMEM_LIMIT),
        name="ffn",
    )(h, g, wg, wu, wd)


def _rope_rows(t1, t2, cos, sin):
    return t1 * cos - t2 * sin, t2 * cos + t1 * sin


def _inproj_kernel(h_ref, g_ref, pos_ref, wt_ref, wn_ref,
                   aqt_ref, iqt_ref, rq_ref, rkt_ref, kk_ref, avt_ref, iwt_ref,
                   rv_ref, rg_ref, ga_ref, gb_ref):
    tm = h_ref.shape[0]
    u = _rms(h_ref[...], g_ref[...]).astype(BF16)

    zn = _dot(u, wn_ref[...])
    rv_ref[...] = zn[:, 0:R_WIDTH].astype(BF16)
    rg_ref[...] = zn[:, R_WIDTH:2 * R_WIDTH].astype(BF16)
    ga_ref[...] = zn[:, 2 * R_WIDTH:2 * R_WIDTH + D_MODEL].astype(BF16)
    gb_ref[...] = zn[:, 2 * R_WIDTH + D_MODEL:].astype(BF16)

    zt = lax.dot_general(wt_ref[...], u, (((1,), (1,)), ((), ())),
                         preferred_element_type=F32)

    pos = pos_ref[0].astype(F32)

    def angles(half):
        i = lax.broadcasted_iota(I32, (half, tm), 0).astype(F32)
        inv = jnp.exp(i * (-math.log(ROPE_THETA) / half))
        ang = pos * inv
        return jnp.cos(ang), jnp.sin(ang)

    cos64, sin64 = angles(A_HEAD_DIM // 2)
    cos128, sin128 = angles(R_QK_DIM // 2)

    def rope_head(row0, dim, cos, sin):
        half = dim // 2
        return _rope_rows(zt[row0:row0 + half], zt[row0 + half:row0 + dim], cos, sin)

    for h in range(A_HEADS):
        o1, o2 = rope_head(_T_AQ + h * A_HEAD_DIM, A_HEAD_DIM, cos64, sin64)
        r = h * A_HEAD_DIM
        aqt_ref[r:r + 32, :] = (o1 * (A_HEAD_DIM ** -0.5 * LOG2_E)).astype(BF16)
        aqt_ref[r + 32:r + 64, :] = (o2 * (A_HEAD_DIM ** -0.5 * LOG2_E)).astype(BF16)
    for h in range(IDX_HEADS):
        o1, o2 = rope_head(_T_IQ + h * IDX_DIM, IDX_DIM, cos64, sin64)
        r = h * IDX_DIM
        iqt_ref[r:r + 32, :] = (o1 * (IDX_DIM ** -0.5)).astype(BF16)
        iqt_ref[r + 32:r + 64, :] = (o2 * (IDX_DIM ** -0.5)).astype(BF16)

    for h in range(R_HEADS):
        o1, o2 = rope_head(_T_RQ + h * R_QK_DIM, R_QK_DIM, cos128, sin128)
        q = jnp.concatenate([o1, o2], axis=0)
        rq_ref[:, h * R_QK_DIM:(h + 1) * R_QK_DIM] = q.T.astype(BF16)
        k1, k2 = rope_head(_T_RK + h * R_QK_DIM, R_QK_DIM, cos128, sin128)
        r = h * R_QK_DIM
        rkt_ref[r:r + 64, :] = (k1 * (R_QK_DIM ** -0.5)).astype(BF16)
        rkt_ref[r + 64:r + 128, :] = (k2 * (R_QK_DIM ** -0.5)).astype(BF16)

    a1, a2 = rope_head(_T_AK, A_HEAD_DIM, cos64, sin64)
    i1, i2 = rope_head(_T_IK, IDX_DIM, cos64, sin64)
    kk = jnp.concatenate([a1, a2, i1, i2], axis=0)
    kk_ref[...] = kk.T.astype(BF16)

    avt_ref[0:A_HEAD_DIM, :] = zt[_T_AV:_T_AV + A_HEAD_DIM].astype(BF16)
    avt_ref[A_HEAD_DIM:AV_ROWS, :] = jnp.ones((AV_ROWS - A_HEAD_DIM, tm), BF16)
    iwt_ref[...] = zt[_T_IW:_T_IW + IDX_HEADS] * (IDX_HEADS ** -0.5)


def _inproj(h, g, pos3, wt, wn, tm):
    t = h.shape[0]
    nt = t // tm
    row = lambda i: (i, 0)
    col = lambda i: (0, i)
    outs = [
        (jax.ShapeDtypeStruct((A_WIDTH, t), BF16), pl.BlockSpec((A_WIDTH, tm), col)),
        (jax.ShapeDtypeStruct((A_WIDTH, t), BF16), pl.BlockSpec((A_WIDTH, tm), col)),
        (jax.ShapeDtypeStruct((t, 512), BF16), pl.BlockSpec((tm, 512), row)),
        (jax.ShapeDtypeStruct((512, t), BF16), pl.BlockSpec((512, tm), col)),
        (jax.ShapeDtypeStruct((t, 128), BF16), pl.BlockSpec((tm, 128), row)),
        (jax.ShapeDtypeStruct((AV_ROWS, t), BF16), pl.BlockSpec((AV_ROWS, tm), col)),
        (jax.ShapeDtypeStruct((IDX_HEADS, t), F32), pl.BlockSpec((IDX_HEADS, tm), col)),
        (jax.ShapeDtypeStruct((t, R_WIDTH), BF16), pl.BlockSpec((tm, R_WIDTH), row)),
        (jax.ShapeDtypeStruct((t, R_WIDTH), BF16), pl.BlockSpec((tm, R_WIDTH), row)),
        (jax.ShapeDtypeStruct((t, D_MODEL), BF16), pl.BlockSpec((tm, D_MODEL), row)),
        (jax.ShapeDtypeStruct((t, D_MODEL), BF16), pl.BlockSpec((tm, D_MODEL), row)),
    ]
    return pl.pallas_call(
        _inproj_kernel,
        grid=(nt,),
        in_specs=[
            pl.BlockSpec((tm, D_MODEL), row),
            _const_spec((1, D_MODEL)),
            pl.BlockSpec((1, 1, tm), lambda i: (i, 0, 0)),
            _const_spec((_T_ROWS, D_MODEL)),
            _const_spec((D_MODEL, _N_TOKEN_MAJOR)),
        ],
        out_specs=[o[1] for o in outs],
        out_shape=[o[0] for o in outs],
        compiler_params=pltpu.CompilerParams(
            dimension_semantics=("parallel",), vmem_limit_bytes=VMEM_LIMIT),
        name="inproj",
    )(h, g, pos3, wt, wn)


def _tree_or(parts):
    while len(parts) > 1:
        parts = [parts[i] | parts[i + 1] for i in range(0, len(parts) - 1, 2)] + (
            [parts[-1]] if len(parts) % 2 else [])
    return parts[0]


_BT_MASKS = {16: 0x0000FFFF, 8: 0x00FF00FF, 4: 0x0F0F0F0F, 2: 0x33333333, 1: 0x55555555}


def _bit_transpose32(rows):
    a = list(rows)
    j = 16
    while j:
        mask = _BT_MASKS[j]
        k = 0
        while k < WORD:
            t = (a[k] ^ (a[k + j] >> j)) & mask
            a[k] = a[k] ^ t
            a[k + j] = a[k + j] ^ (t << j)
            k = (k + j + 1) & ~j
        j >>= 1
    return a


def _word_bit(m):
    v = 1 << (WORD - 1 - m)
    return v - (1 << WORD) if v >= (1 << (WORD - 1)) else v


def _dsa_kernel(kk_ref, avt_ref, aqt_ref, iqt_ref, iwt_ref, o_ref,
                lg_ref, planes_ref, posp_ref, valid_ref, sel_ref, qa_ref, qi_ref,
                acc_ref, ml_ref, *, n_sel, seq):
    nq = DSA_QB
    words = seq // WORD
    wpt = KEY_TILE // WORD
    pos_bits = (seq - 1).bit_length()

    @pl.when(pl.program_id(0) == 0)
    def _():
        qa_ref[64:128, :] = jnp.zeros((64, A_HEADS * nq), BF16)
        qi_ref[0:64, :] = jnp.zeros((64, A_HEADS * nq), BF16)
        planes_ref[...] = jnp.zeros(planes_ref.shape, I32)
        valid_ref[...] = jnp.zeros(valid_ref.shape, I32)
        for t in range(seq // KEY_TILE):
            sub = lax.broadcasted_iota(I32, (wpt, nq), 0)
            p = _bit_transpose32([(seq - 1 - t * KEY_TILE - m * wpt) - sub for m in range(WORD)])
            for i in range(pos_bits):
                posp_ref[i, t * wpt:(t + 1) * wpt, :] = p[WORD - pos_bits + i]

    def query_block(j, carry):
        _dsa_query_block(j, kk_ref, avt_ref, aqt_ref, iqt_ref, iwt_ref, o_ref,
                         lg_ref, planes_ref, posp_ref, valid_ref, sel_ref, qa_ref, qi_ref,
                         acc_ref, ml_ref, n_sel=n_sel, seq=seq)
        return carry

    lax.fori_loop(0, seq // nq, query_block, 0)


def _dsa_query_block(j, kk_ref, avt_ref, aqt_ref, iqt_ref, iwt_ref, o_ref,
                     lg_ref, planes_ref, posp_ref, valid_ref, sel_ref, qa_ref, qi_ref,
                     acc_ref, ml_ref, *, n_sel, seq):
    nq = DSA_QB
    n_tiles = j + 1
    words = seq // WORD
    wpt = KEY_TILE // WORD
    pos_bits = (seq - 1).bit_length()
    queries = pl.ds(pl.multiple_of(j * nq, nq), nq)

    for h in range(A_HEADS):
        cs = slice(h * nq, (h + 1) * nq)
        qa_ref[0:64, cs] = aqt_ref[h * 64:(h + 1) * 64, queries]
        qi_ref[64:128, cs] = iqt_ref[h * 64:(h + 1) * 64, queries]

    lane = lax.broadcasted_iota(I32, (1, nq), 1)
    vis_end = (j * nq + (lane // CHUNK + 1) * CHUNK)
    w = iwt_ref[:, queries]

    def tile_rows(t):
        return pl.ds(pl.multiple_of(t * KEY_TILE, KEY_TILE), KEY_TILE)

    def word_rows(t):
        return pl.ds(pl.multiple_of(t * wpt, wpt), wpt)

    def for_each_tile(body):
        def group(i, carry):
            for u in range(TILE_UNROLL):
                body(TILE_UNROLL * i + u)
            return carry

        lax.fori_loop(0, n_tiles // TILE_UNROLL, group, 0)
        done = n_tiles - n_tiles % TILE_UNROLL
        width = TILE_UNROLL // 2
        while width:
            @pl.when((n_tiles // width) % 2 == 1)
            def _(done=done, width=width):
                for u in range(width):
                    body(done + u)
            done = done + (n_tiles // width) % 2 * width
            width //= 2

    ml_ref[1:2, :] = jnp.full((1, A_HEADS * nq), 0.5 * NEG_BIG, F32)

    def score_tile(t):
        keys, valid = [], []
        for half in range(2):
            r0 = pl.multiple_of(t * KEY_TILE + half * KEY_HALF, KEY_HALF)
            kt = kk_ref[pl.ds(r0, KEY_HALF), :]
            sc = None
            for h in range(IDX_HEADS):
                rel = jnp.maximum(_dot(kt, qi_ref[:, h * nq:(h + 1) * nq]), 0.0)
                part = rel * w[h:h + 1, :]
                sc = part if sc is None else sc + part
            sc = jnp.where(sc == 0.0, 0.0, sc)
            spos = r0 + lax.broadcasted_iota(I32, (KEY_HALF, nq), 0)
            sc = jnp.where(spos < vis_end, sc, NEG_BIG)
            bits = pltpu.bitcast(sc, I32)
            keys.append(bits ^ ((bits >> 31) | jnp.int32(INT_MIN)))
            valid.append(sc > 0.5 * NEG_BIG)

        kt_all = kk_ref[tile_rows(t), :]
        for h in range(A_HEADS):
            cs = slice(h * nq, (h + 1) * nq)
            lg = _dot(kt_all, qa_ref[:, cs])
            lg_ref[tile_rows(t), cs] = lg
            ml_ref[1:2, cs] = jnp.maximum(ml_ref[1:2, cs], jnp.max(lg, axis=0, keepdims=True))

        def member(parts, m):
            r = (m % 16) * 8
            return parts[m // 16][r:r + 8, :]

        planes = _bit_transpose32([member(keys, m) for m in range(WORD)])
        for i in range(WORD):
            planes_ref[i, word_rows(t), :] = planes[i]
        valid_ref[word_rows(t), :] = _tree_or(
            [jnp.where(member(valid, m), jnp.int32(_word_bit(m)), jnp.int32(0)) for m in range(WORD)])

    for_each_tile(score_tile)

    wrow = lax.broadcasted_iota(I32, (words, nq), 0)
    state = (jnp.where(wrow < n_tiles * wpt, jnp.int32(-1), jnp.int32(0)),
             jnp.zeros((words, nq), I32),
             jnp.zeros((1, nq), I32))

    def radix_step(plane, state):
        alive, sel, above = state
        ones = alive & plane
        c = jnp.sum(lax.population_count(ones), axis=0, keepdims=True)
        keep_ones = (above + c) >= n_sel
        return (jnp.where(keep_ones, ones, alive ^ ones),
                jnp.where(keep_ones, sel, sel | ones),
                jnp.where(keep_ones, above, above + c))

    state = lax.fori_loop(0, WORD, lambda i, s: radix_step(planes_ref[i], s), state)
    state = lax.fori_loop(0, pos_bits, lambda i, s: radix_step(posp_ref[i], s), state)
    alive, sel, _ = state
    sel_ref[...] = (sel | alive) & valid_ref[...]

    def masked_logits(t, cs, bias):
        return lg_ref[tile_rows(t), cs] + bias

    def mask_bias(t):
        selw = sel_ref[word_rows(t), :]
        return jnp.concatenate(
            [jnp.where((selw << m) < 0, 0.0, NEG_BIG) for m in range(WORD)], axis=0)

    def attend(shift_row):
        acc_ref[...] = jnp.zeros(acc_ref.shape, F32)

        def attn_tile(t):
            vt = avt_ref[:, tile_rows(t)]
            bias = mask_bias(t)
            for h in range(A_HEADS):
                cs = slice(h * nq, (h + 1) * nq)
                p = jnp.exp2(masked_logits(t, cs, bias) - ml_ref[shift_row:shift_row + 1, cs])
                acc_ref[:, cs] = acc_ref[:, cs] + _dot(vt, p.astype(BF16))

        for_each_tile(attn_tile)

    attend(1)
    underflow = jnp.min(acc_ref[A_HEAD_DIM:A_HEAD_DIM + 1, :]) < DENOM_MIN

    @pl.when(underflow)
    def _():
        ml_ref[0:1, :] = jnp.full((1, A_HEADS * nq), 0.5 * NEG_BIG, F32)

        def max_tile(t, carry):
            bias = mask_bias(t)
            for h in range(A_HEADS):
                cs = slice(h * nq, (h + 1) * nq)
                ml_ref[0:1, cs] = jnp.maximum(
                    ml_ref[0:1, cs], jnp.max(masked_logits(t, cs, bias), axis=0, keepdims=True))
            return carry

        lax.fori_loop(0, n_tiles, max_tile, 0)
        attend(0)

    out_t = acc_ref[0:A_HEAD_DIM, :] / acc_ref[A_HEAD_DIM:A_HEAD_DIM + 1, :]
    for hp in range(A_HEADS // 2):
        blk = jnp.concatenate(
            [out_t[:, (2 * hp) * nq:(2 * hp + 1) * nq], out_t[:, (2 * hp + 1) * nq:(2 * hp + 2) * nq]], axis=0)
        o_ref[queries, hp * 128:(hp + 1) * 128] = blk.T.astype(o_ref.dtype)


def _dsa(kk, avt, aqt, iqt, iwt, batch, seq, n_sel):
    t = batch * seq
    feat = lambda b: (0, b)
    return pl.pallas_call(
        functools.partial(_dsa_kernel, n_sel=n_sel, seq=seq),
        grid=(batch,),
        in_specs=[
            pl.BlockSpec((seq, 128), lambda b: (b, 0)),
            pl.BlockSpec((AV_ROWS, seq), feat),
            pl.BlockSpec((A_WIDTH, seq), feat),
            pl.BlockSpec((A_WIDTH, seq), feat),
            pl.BlockSpec((IDX_HEADS, seq), feat),
        ],
        out_specs=pl.BlockSpec((seq, A_WIDTH), lambda b: (b, 0)),
        out_shape=jax.ShapeDtypeStruct((t, A_WIDTH), BF16),
        scratch_shapes=[
            pltpu.VMEM((seq, A_HEADS * DSA_QB), F32),
            pltpu.VMEM((WORD, seq // WORD, DSA_QB), I32),
            pltpu.VMEM(((seq - 1).bit_length(), seq // WORD, DSA_QB), I32),
            pltpu.VMEM((seq // WORD, DSA_QB), I32),
            pltpu.VMEM((seq // WORD, DSA_QB), I32),
            pltpu.VMEM((128, A_HEADS * DSA_QB), BF16),
            pltpu.VMEM((128, A_HEADS * DSA_QB), BF16),
            pltpu.VMEM((AV_ROWS, A_HEADS * DSA_QB), F32),
            pltpu.VMEM((8, A_HEADS * DSA_QB), F32),
        ],
        compiler_params=pltpu.CompilerParams(
            dimension_semantics=("arbitrary",), vmem_limit_bytes=VMEM_LIMIT),
        name="dsa",
    )(kk, avt, aqt, iqt, iwt)


def _log_gamma(h):
    return math.log1p(-(2.0 ** (-5.0 - h)))


def _retention_kernel(rq_ref, rkt_ref, rv_ref, rg_ref, gn_ref, o_ref,
                      state_ref, dmat_ref, dq_ref):
    nb = RET_BLOCK
    first = (pl.program_id(0) == 0) & (pl.program_id(1) == 0)

    @pl.when(first)
    def _():
        i = lax.broadcasted_iota(I32, (nb, nb), 0)
        jj = lax.broadcasted_iota(I32, (nb, nb), 1)
        dist = jnp.abs(i - jj).astype(F32)
        seen = (jj // CHUNK) <= (i // CHUNK)
        for h in range(R_HEADS):
            lg = _log_gamma(h)
            dmat_ref[h] = jnp.where(seen, jnp.exp(lg * dist), 0.0)
            dq_ref[h] = jnp.exp(lg * (i.astype(F32) + 1.0))

    @pl.when(pl.program_id(1) == 0)
    def _():
        state_ref[...] = jnp.zeros(state_ref.shape, F32)

    jrow = lax.broadcasted_iota(I32, (1, nb), 1).astype(F32)
    for h in range(R_HEADS):
        lg = _log_gamma(h)
        q = rq_ref[:, h * R_QK_DIM:(h + 1) * R_QK_DIM]
        kt = rkt_ref[h * R_QK_DIM:(h + 1) * R_QK_DIM, :]
        v = rv_ref[:, h * R_V_DIM:(h + 1) * R_V_DIM]
        state = state_ref[h]
        s = _dot(q, kt) * dmat_ref[h]
        y = _dot(s.astype(BF16), v) + _dot(q, state.astype(BF16)) * dq_ref[h]
        dk = jnp.exp(lg * (nb - 1.0 - jrow))
        kd = (kt.astype(F32) * dk).astype(BF16)
        state_ref[h] = state * math.exp(lg * nb) + _dot(kd, v)
        mean = jnp.mean(y, axis=-1, keepdims=True)
        yc = y - mean
        var = jnp.mean(yc * yc, axis=-1, keepdims=True)
        yn = yc * lax.rsqrt(var + GN_EPS) * gn_ref[:, h * R_V_DIM:(h + 1) * R_V_DIM]
        gate = rg_ref[:, h * R_V_DIM:(h + 1) * R_V_DIM].astype(F32)
        o_ref[:, h * R_V_DIM:(h + 1) * R_V_DIM] = (yn * (gate * jax.nn.sigmoid(gate))).astype(o_ref.dtype)


def _retention(rq, rkt, rv, rg, gn, batch, seq):
    t = batch * seq
    nblk = seq // RET_BLOCK
    tok = lambda b, n: (b * nblk + n, 0)
    return pl.pallas_call(
        _retention_kernel,
        grid=(batch, nblk),
        in_specs=[
            pl.BlockSpec((RET_BLOCK, 512), tok),
            pl.BlockSpec((512, RET_BLOCK), lambda b, n: (0, b * nblk + n)),
            pl.BlockSpec((RET_BLOCK, R_WIDTH), tok),
            pl.BlockSpec((RET_BLOCK, R_WIDTH), tok),
            pl.BlockSpec((1, R_WIDTH), lambda b, n: (0, 0)),
        ],
        out_specs=pl.BlockSpec((RET_BLOCK, R_WIDTH), tok),
        out_shape=jax.ShapeDtypeStruct((t, R_WIDTH), BF16),
        scratch_shapes=[
            pltpu.VMEM((R_HEADS, R_QK_DIM, R_V_DIM), F32),
            pltpu.VMEM((R_HEADS, RET_BLOCK, RET_BLOCK), F32),
            pltpu.VMEM((R_HEADS, RET_BLOCK, RET_BLOCK), F32),
        ],
        compiler_params=pltpu.CompilerParams(
            dimension_semantics=("arbitrary", "arbitrary"), vmem_limit_bytes=VMEM_LIMIT),
        name="retention",
    )(rq, rkt, rv, rg, gn)


def _tail_kernel(h_ref, ya_ref, yr_ref, ga_ref, gb_ref, p_ref,
                 wa_ref, wb_ref, wo_ref, g2_ref, wg_ref, wu_ref, wd_ref,
                 gp_ref, wpg_ref, wpp_ref, gf_ref, o_ref):
    a = _dot(ya_ref[...], wa_ref[...])
    b = _dot(yr_ref[...], wb_ref[...])
    merged = (jax.nn.sigmoid(ga_ref[...].astype(F32)) * a
              + jax.nn.sigmoid(gb_ref[...].astype(F32)) * b)
    h = h_ref[...] + _dot(merged.astype(BF16), wo_ref[...])
    h = h + 0.5 * _swiglu(_rms(h, g2_ref[...]).astype(BF16), wg_ref, wu_ref, wd_ref)
    gate = jax.nn.sigmoid(_dot(_rms(h, gp_ref[...]).astype(BF16), wpg_ref[...]))
    emb = _dot(p_ref[...].astype(BF16), wpp_ref[...])
    o_ref[...] = _rms(h + gate * emb, gf_ref[...])


def _tail(h, ya, yr, ga, gb, p, wa, wb, wo, g2, wg, wu, wd, gp, wpg, wpp, gf, tm):
    t = h.shape[0]
    row = lambda i: (i, 0)
    return pl.pallas_call(
        _tail_kernel,
        grid=(t // tm,),
        in_specs=[
            pl.BlockSpec((tm, D_MODEL), row),
            pl.BlockSpec((tm, A_WIDTH), row),
            pl.BlockSpec((tm, R_WIDTH), row),
            pl.BlockSpec((tm, D_MODEL), row),
            pl.BlockSpec((tm, D_MODEL), row),
            pl.BlockSpec((tm, PLE_DIM), row),
            _const_spec((A_WIDTH, D_MODEL)),
            _const_spec((R_WIDTH, D_MODEL)),
            _const_spec((D_MODEL, D_MODEL)),
            _const_spec((1, D_MODEL)),
            _const_spec((D_MODEL, D_FF)),
            _const_spec((D_MODEL, D_FF)),
            _const_spec((D_FF, D_MODEL)),
            _const_spec((1, D_MODEL)),
            _const_spec((D_MODEL, D_MODEL)),
            _const_spec((PLE_DIM, D_MODEL)),
            _const_spec((1, D_MODEL)),
        ],
        out_specs=pl.BlockSpec((tm, D_MODEL), row),
        out_shape=jax.ShapeDtypeStruct((t, D_MODEL), F32),
        compiler_params=pltpu.CompilerParams(
            dimension_semantics=("parallel",), vmem_limit_bytes=VMEM_LIMIT),
        name="tail",
    )(h, ya, yr, ga, gb, p, wa, wb, wo, g2, wg, wu, wd, gp, wpg, wpp, gf)


def _token_tile(t):
    return 512 if t % 512 == 0 else 256


def kernel(x, p, positions, ffn1_norm, ffn1_w_gate, ffn1_w_up, ffn1_w_down, mix_norm, w_in, ret_gn, w_branch_a, w_branch_b, w_out, ffn2_norm, ffn2_w_gate, ffn2_w_up, ffn2_w_down, ple_norm, w_ple_gate, w_ple_proj, final_norm):
    batch, seq, _ = x.shape
    depth = p.shape[0]
    t = batch * seq
    tm = _token_tile(t)
    n_sel = min(TOPK_MAX, seq // 4)
    assert depth == 1, "the final norm is fused into the per-layer embedding step"
    assert seq % RET_BLOCK == 0 and seq % KEY_TILE == 0 and seq % DSA_QB == 0
    assert n_sel <= KEY_TILE and DSA_QB == KEY_TILE and KEY_TILE == 8 * WORD

    h = x.reshape(t, D_MODEL)
    pos3 = positions.reshape(t // tm, 1, tm)
    vec = lambda g: g.reshape(1, -1).astype(F32)

    for i in range(depth):
        w = w_in[i].astype(BF16)
        wt = jnp.concatenate([
            w[:, _OFF_AQ:_OFF_AQ + A_WIDTH], w[:, _OFF_IQ:_OFF_IQ + IDX_HEADS * IDX_DIM],
            w[:, _OFF_RQ:_OFF_RQ + 512], w[:, _OFF_RK:_OFF_RK + 512],
            w[:, _OFF_AK:_OFF_AK + 64], w[:, _OFF_IK:_OFF_IK + 64],
            w[:, _OFF_AV:_OFF_AV + 64], w[:, _OFF_IW:_OFF_IW + IDX_HEADS]], axis=1).T
        wn = w[:, _OFF_RV:]

        h = _ffn(h, vec(ffn1_norm[i]), ffn1_w_gate[i].astype(BF16), ffn1_w_up[i].astype(BF16),
                 ffn1_w_down[i].astype(BF16), tm)
        aqt, iqt, rq, rkt, kk, avt, iwt, rv, rg, ga, gb = _inproj(h, vec(mix_norm[i]), pos3, wt, wn, tm)
        ya = _dsa(kk, avt, aqt, iqt, iwt, batch, seq, n_sel)
        yr = _retention(rq, rkt, rv, rg, vec(ret_gn[i]), batch, seq)
        h = _tail(h, ya, yr, ga, gb, p[i].reshape(t, PLE_DIM),
                  w_branch_a[i].astype(BF16), w_branch_b[i].astype(BF16), w_out[i].astype(BF16),
                  vec(ffn2_norm[i]), ffn2_w_gate[i].astype(BF16), ffn2_w_up[i].astype(BF16),
                  ffn2_w_down[i].astype(BF16),
                  vec(ple_norm[i]), w_ple_gate[i].astype(BF16), w_ple_proj[i].astype(BF16),
                  vec(final_norm), tm)
    return h.reshape(batch, seq, D_MODEL)
```

```python
import functools
import math

import jax
import jax.numpy as jnp
import numpy as np
from jax import lax
from jax.experimental import pallas as pl
from jax.experimental.pallas import tpu as pltpu

F32 = jnp.float32
BF16 = jnp.bfloat16
I32 = jnp.int32

D_MODEL = 1024
CHUNK = 64
ROPE_THETA = 10000.0
EPS = 1e-6
GN_EPS = 1e-5
NEG_BIG = -1e30
A_HEADS = 8
A_HEAD_DIM = 64
IDX_HEADS = 8
IDX_DIM = 64
TOPK_MAX = 256
A_WIDTH = A_HEADS * A_HEAD_DIM
R_HEADS = 4
R_QK_DIM = 128
R_V_DIM = 256
R_WIDTH = R_HEADS * R_V_DIM
D_FF = 2816
PLE_DIM = 256

_OFF_AQ = 0
_OFF_AK = _OFF_AQ + A_WIDTH
_OFF_AV = _OFF_AK + A_HEAD_DIM
_OFF_IQ = _OFF_AV + A_HEAD_DIM
_OFF_IK = _OFF_IQ + IDX_HEADS * IDX_DIM
_OFF_IW = _OFF_IK + IDX_DIM
_OFF_RQ = _OFF_IW + IDX_HEADS
_OFF_RK = _OFF_RQ + R_HEADS * R_QK_DIM
_OFF_RV = _OFF_RK + R_HEADS * R_QK_DIM
_N_TOKEN_MAJOR = 2 * R_WIDTH + 2 * D_MODEL

_T_AQ = 0
_T_IQ = _T_AQ + A_WIDTH
_T_RQ = _T_IQ + IDX_HEADS * IDX_DIM
_T_RK = _T_RQ + R_HEADS * R_QK_DIM
_T_AK = _T_RK + R_HEADS * R_QK_DIM
_T_IK = _T_AK + A_HEAD_DIM
_T_AV = _T_IK + IDX_DIM
_T_IW = _T_AV + A_HEAD_DIM
_T_ROWS = _T_IW + IDX_HEADS

VMEM_LIMIT = 56 * 1024 * 1024
FF_CHUNK = 256
DSA_QB = 256
KEY_TILE = 256
KEY_HALF = KEY_TILE // 2
TILE_UNROLL = 4
RET_BLOCK = 256

AV_ROWS = A_HEAD_DIM + 16
INT_MIN = -(2 ** 31)
WORD = 32
LOG2_E = math.log2(math.e)
DENOM_MIN = 2.0 ** -100


def _const_spec(shape):
    nd = len(shape)
    return pl.BlockSpec(shape, lambda *_: (0,) * nd, pipeline_mode=pl.Buffered(1))


def _rms(x, g):
    ms = jnp.mean(x * x, axis=-1, keepdims=True)
    return x * lax.rsqrt(ms + EPS) * g


def _dot(a, b):
    return jnp.dot(a, b, preferred_element_type=F32)


def _swiglu(u, wg_ref, wu_ref, wd_ref):
    acc = jnp.zeros((u.shape[0], D_MODEL), F32)
    for c in range(D_FF // FF_CHUNK):
        sl = slice(c * FF_CHUNK, (c + 1) * FF_CHUNK)
        gate = _dot(u, wg_ref[:, sl])
        up = _dot(u, wu_ref[:, sl])
        act = (gate * jax.nn.sigmoid(gate) * up).astype(BF16)
        acc = acc + _dot(act, wd_ref[sl, :])
    return acc


def _ffn_kernel(h_ref, g_ref, wg_ref, wu_ref, wd_ref, o_ref):
    h = h_ref[...]
    o_ref[...] = h + 0.5 * _swiglu(_rms(h, g_ref[...]).astype(BF16), wg_ref, wu_ref, wd_ref)


def _ffn(h, g, wg, wu, wd, tm):
    t = h.shape[0]
    return pl.pallas_call(
        _ffn_kernel,
        grid=(t // tm,),
        in_specs=[
            pl.BlockSpec((tm, D_MODEL), lambda i: (i, 0)),
            _const_spec((1, D_MODEL)),
            _const_spec((D_MODEL, D_FF)),
            _const_spec((D_MODEL, D_FF)),
            _const_spec((D_FF, D_MODEL)),
        ],
        out_specs=pl.BlockSpec((tm, D_MODEL), lambda i: (i, 0)),
        out_shape=jax.ShapeDtypeStruct((t, D_MODEL), F32),
        compiler_params=pltpu.CompilerParams(
            dimension_semantics=("parallel",), vmem_limit_bytes=VMEM_LIMIT),
        name="ffn",
    )(h, g, wg, wu, wd)


def _rope_rows(t1, t2, cos, sin):
    return t1 * cos - t2 * sin, t2 * cos + t1 * sin


def _inproj_kernel(h_ref, g_ref, pos_ref, wt_ref, wn_ref,
                   aqt_ref, iqt_ref, rq_ref, rkt_ref, kk_ref, avt_ref, iwt_ref,
                   rv_ref, rg_ref, ga_ref, gb_ref):
    tm = h_ref.shape[0]
    u = _rms(h_ref[...], g_ref[...]).astype(BF16)

    zn = _dot(u, wn_ref[...])
    rv_ref[...] = zn[:, 0:R_WIDTH].astype(BF16)
    rg_ref[...] = zn[:, R_WIDTH:2 * R_WIDTH].astype(BF16)
    ga_ref[...] = zn[:, 2 * R_WIDTH:2 * R_WIDTH + D_MODEL].astype(BF16)
    gb_ref[...] = zn[:, 2 * R_WIDTH + D_MODEL:].astype(BF16)

    zt = lax.dot_general(wt_ref[...], u, (((1,), (1,)), ((), ())),
                         preferred_element_type=F32)

    pos = pos_ref[0].astype(F32)

    def angles(half):
        i = lax.broadcasted_iota(I32, (half, tm), 0).astype(F32)
        inv = jnp.exp(i * (-math.log(ROPE_THETA) / half))
        ang = pos * inv
        return jnp.cos(ang), jnp.sin(ang)

    cos64, sin64 = angles(A_HEAD_DIM // 2)
    cos128, sin128 = angles(R_QK_DIM // 2)

    def rope_head(row0, dim, cos, sin):
        half = dim // 2
        return _rope_rows(zt[row0:row0 + half], zt[row0 + half:row0 + dim], cos, sin)

    for h in range(A_HEADS):
        o1, o2 = rope_head(_T_AQ + h * A_HEAD_DIM, A_HEAD_DIM, cos64, sin64)
        r = h * A_HEAD_DIM
        aqt_ref[r:r + 32, :] = (o1 * (A_HEAD_DIM ** -0.5 * LOG2_E)).astype(BF16)
        aqt_ref[r + 32:r + 64, :] = (o2 * (A_HEAD_DIM ** -0.5 * LOG2_E)).astype(BF16)
    for h in range(IDX_HEADS):
        o1, o2 = rope_head(_T_IQ + h * IDX_DIM, IDX_DIM, cos64, sin64)
        r = h * IDX_DIM
        iqt_ref[r:r + 32, :] = (o1 * (IDX_DIM ** -0.5)).astype(BF16)
        iqt_ref[r + 32:r + 64, :] = (o2 * (IDX_DIM ** -0.5)).astype(BF16)

    for h in range(R_HEADS):
        o1, o2 = rope_head(_T_RQ + h * R_QK_DIM, R_QK_DIM, cos128, sin128)
        q = jnp.concatenate([o1, o2], axis=0)
        rq_ref[:, h * R_QK_DIM:(h + 1) * R_QK_DIM] = q.T.astype(BF16)
        k1, k2 = rope_head(_T_RK + h * R_QK_DIM, R_QK_DIM, cos128, sin128)
        r = h * R_QK_DIM
        rkt_ref[r:r + 64, :] = (k1 * (R_QK_DIM ** -0.5)).astype(BF16)
        rkt_ref[r + 64:r + 128, :] = (k2 * (R_QK_DIM ** -0.5)).astype(BF16)

    a1, a2 = rope_head(_T_AK, A_HEAD_DIM, cos64, sin64)
    i1, i2 = rope_head(_T_IK, IDX_DIM, cos64, sin64)
    kk = jnp.concatenate([a1, a2, i1, i2], axis=0)
    kk_ref[...] = kk.T.astype(BF16)

    avt_ref[0:A_HEAD_DIM, :] = zt[_T_AV:_T_AV + A_HEAD_DIM].astype(BF16)
    avt_ref[A_HEAD_DIM:AV_ROWS, :] = jnp.ones((AV_ROWS - A_HEAD_DIM, tm), BF16)
    iwt_ref[...] = zt[_T_IW:_T_IW + IDX_HEADS] * (IDX_HEADS ** -0.5)


def _inproj(h, g, pos3, wt, wn, tm):
    t = h.shape[0]
    nt = t // tm
    row = lambda i: (i, 0)
    col = lambda i: (0, i)
    outs = [
        (jax.ShapeDtypeStruct((A_WIDTH, t), BF16), pl.BlockSpec((A_WIDTH, tm), col)),
        (jax.ShapeDtypeStruct((A_WIDTH, t), BF16), pl.BlockSpec((A_WIDTH, tm), col)),
        (jax.ShapeDtypeStruct((t, 512), BF16), pl.BlockSpec((tm, 512), row)),
        (jax.ShapeDtypeStruct((512, t), BF16), pl.BlockSpec((512, tm), col)),
        (jax.ShapeDtypeStruct((t, 128), BF16), pl.BlockSpec((tm, 128), row)),
        (jax.ShapeDtypeStruct((AV_ROWS, t), BF16), pl.BlockSpec((AV_ROWS, tm), col)),
        (jax.ShapeDtypeStruct((IDX_HEADS, t), F32), pl.BlockSpec((IDX_HEADS, tm), col)),
        (jax.ShapeDtypeStruct((t, R_WIDTH), BF16), pl.BlockSpec((tm, R_WIDTH), row)),
        (jax.ShapeDtypeStruct((t, R_WIDTH), BF16), pl.BlockSpec((tm, R_WIDTH), row)),
        (jax.ShapeDtypeStruct((t, D_MODEL), BF16), pl.BlockSpec((tm, D_MODEL), row)),
        (jax.ShapeDtypeStruct((t, D_MODEL), BF16), pl.BlockSpec((tm, D_MODEL), row)),
    ]
    return pl.pallas_call(
        _inproj_kernel,
        grid=(nt,),
        in_specs=[
            pl.BlockSpec((tm, D_MODEL), row),
            _const_spec((1, D_MODEL)),
            pl.BlockSpec((1, 1, tm), lambda i: (i, 0, 0)),
            _const_spec((_T_ROWS, D_MODEL)),
            _const_spec((D_MODEL, _N_TOKEN_MAJOR)),
        ],
        out_specs=[o[1] for o in outs],
        out_shape=[o[0] for o in outs],
        compiler_params=pltpu.CompilerParams(
            dimension_semantics=("parallel",), vmem_limit_bytes=VMEM_LIMIT),
        name="inproj",
    )(h, g, pos3, wt, wn)


def _tree_or(parts):
    while len(parts) > 1:
        parts = [parts[i] | parts[i + 1] for i in range(0, len(parts) - 1, 2)] + (
            [parts[-1]] if len(parts) % 2 else [])
    return parts[0]


_BT_MASKS = {16: 0x0000FFFF, 8: 0x00FF00FF, 4: 0x0F0F0F0F, 2: 0x33333333, 1: 0x55555555}


def _bit_transpose32(rows):
    a = list(rows)
    j = 16
    while j:
        mask = _BT_MASKS[j]
        k = 0
        while k < WORD:
            t = (a[k] ^ (a[k + j] >> j)) & mask
            a[k] = a[k] ^ t
            a[k + j] = a[k + j] ^ (t << j)
            k = (k + j + 1) & ~j
        j >>= 1
    return a


def _word_bit(m):
    v = 1 << (WORD - 1 - m)
    return v - (1 << WORD) if v >= (1 << (WORD - 1)) else v


def _dsa_kernel(kk_ref, avt_ref, aqt_ref, iqt_ref, iwt_ref, o_ref,
                lg_ref, planes_ref, posp_ref, valid_ref, sel_ref, qa_ref, qi_ref,
                acc_ref, ml_ref, *, n_sel, seq):
    nq = DSA_QB
    words = seq // WORD
    wpt = KEY_TILE // WORD
    pos_bits = (seq - 1).bit_length()

    @pl.when(pl.program_id(0) == 0)
    def _():
        qa_ref[64:128, :] = jnp.zeros((64, A_HEADS * nq), BF16)
        qi_ref[0:64, :] = jnp.zeros((64, A_HEADS * nq), BF16)
        planes_ref[...] = jnp.zeros(planes_ref.shape, I32)
        valid_ref[...] = jnp.zeros(valid_ref.shape, I32)
        for t in range(seq // KEY_TILE):
            sub = lax.broadcasted_iota(I32, (wpt, nq), 0)
            p = _bit_transpose32([(seq - 1 - t * KEY_TILE - m * wpt) - sub for m in range(WORD)])
            for i in range(pos_bits):
                posp_ref[i, t * wpt:(t + 1) * wpt, :] = p[WORD - pos_bits + i]

    n_blocks = seq // nq

    def block_pair(p, carry):
        blocks = ((p, 0, 0), (n_blocks - 1 - p, 1, (p + 1) * KEY_TILE))
        for j, slot, lg0 in blocks:
            _dsa_scores(j, slot, lg0, kk_ref, aqt_ref, iqt_ref, iwt_ref,
                        lg_ref, planes_ref, valid_ref, qa_ref, qi_ref, ml_ref)

        states = tuple(_radix_init(j + 1, words, nq) for j, _, _ in blocks)
        states = lax.fori_loop(
            0, WORD, lambda i, st: tuple(_radix_step(planes_ref[b[1], i], s, n_sel) for b, s in zip(blocks, st)),
            states)
        states = lax.fori_loop(
            0, pos_bits, lambda i, st: tuple(_radix_step(posp_ref[i], s, n_sel) for s in st), states)
        for (_, slot, _), (alive, sel, _) in zip(blocks, states):
            sel_ref[slot] = (sel | alive) & valid_ref[slot]

        for j, slot, lg0 in blocks:
            _dsa_attend(j, slot, lg0, avt_ref, o_ref, lg_ref, sel_ref, acc_ref, ml_ref)
        return carry

    lax.fori_loop(0, n_blocks // 2, block_pair, 0)


def _radix_init(n_tiles, words, nq):
    wrow = lax.broadcasted_iota(I32, (words, nq), 0)
    return (jnp.where(wrow < n_tiles * (KEY_TILE // WORD), jnp.int32(-1), jnp.int32(0)),
            jnp.zeros((words, nq), I32),
            jnp.zeros((1, nq), I32))


def _radix_step(plane, state, n_sel):
    alive, sel, above = state
    ones = alive & plane
    c = jnp.sum(lax.population_count(ones), axis=0, keepdims=True)
    keep_ones = (above + c) >= n_sel
    return (jnp.where(keep_ones, ones, alive ^ ones),
            jnp.where(keep_ones, sel, sel | ones),
            jnp.where(keep_ones, above, above + c))


def _tile_helpers(n_tiles, lg0):
    wpt = KEY_TILE // WORD

    def tile_rows(t):
        return pl.ds(pl.multiple_of(t * KEY_TILE, KEY_TILE), KEY_TILE)

    def lg_rows(t):
        return pl.ds(pl.multiple_of(lg0 + t * KEY_TILE, KEY_TILE), KEY_TILE)

    def word_rows(t):
        return pl.ds(pl.multiple_of(t * wpt, wpt), wpt)

    def for_each_tile(body):
        def group(i, carry):
            for u in range(TILE_UNROLL):
                body(TILE_UNROLL * i + u)
            return carry

        lax.fori_loop(0, n_tiles // TILE_UNROLL, group, 0)
        done = n_tiles - n_tiles % TILE_UNROLL
        width = TILE_UNROLL // 2
        while width:
            @pl.when((n_tiles // width) % 2 == 1)
            def _(done=done, width=width):
                for u in range(width):
                    body(done + u)
            done = done + (n_tiles // width) % 2 * width
            width //= 2

    return tile_rows, lg_rows, word_rows, for_each_tile


def _dsa_scores(j, slot, lg0, kk_ref, aqt_ref, iqt_ref, iwt_ref,
                lg_ref, planes_ref, valid_ref, qa_ref, qi_ref, ml_ref):
    nq = DSA_QB
    tile_rows, lg_rows, word_rows, for_each_tile = _tile_helpers(j + 1, lg0)
    queries = pl.ds(pl.multiple_of(j * nq, nq), nq)

    for h in range(A_HEADS):
        cs = slice(h * nq, (h + 1) * nq)
        qa_ref[0:64, cs] = aqt_ref[h * 64:(h + 1) * 64, queries]
        qi_ref[64:128, cs] = iqt_ref[h * 64:(h + 1) * 64, queries]

    lane = lax.broadcasted_iota(I32, (1, nq), 1)
    vis_end = (j * nq + (lane // CHUNK + 1) * CHUNK)
    w = iwt_ref[:, queries]
    ml_ref[slot, 1:2, :] = jnp.full((1, A_HEADS * nq), 0.5 * NEG_BIG, F32)

    def score_tile(t):
        keys, valid = [], []
        for half in range(2):
            r0 = pl.multiple_of(t * KEY_TILE + half * KEY_HALF, KEY_HALF)
            kt = kk_ref[pl.ds(r0, KEY_HALF), :]
            sc = None
            for h in range(IDX_HEADS):
                rel = jnp.maximum(_dot(kt, qi_ref[:, h * nq:(h + 1) * nq]), 0.0)
                part = rel * w[h:h + 1, :]
                sc = part if sc is None else sc + part
            sc = jnp.where(sc == 0.0, 0.0, sc)
            spos = r0 + lax.broadcasted_iota(I32, (KEY_HALF, nq), 0)
            sc = jnp.where(spos < vis_end, sc, NEG_BIG)
            bits = pltpu.bitcast(sc, I32)
            keys.append(bits ^ ((bits >> 31) | jnp.int32(INT_MIN)))
            valid.append(sc > 0.5 * NEG_BIG)

        kt_all = kk_ref[tile_rows(t), :]
        for h in range(A_HEADS):
            cs = slice(h * nq, (h + 1) * nq)
            lg = _dot(kt_all, qa_ref[:, cs])
            lg_ref[lg_rows(t), cs] = lg
            ml_ref[slot, 1:2, cs] = jnp.maximum(ml_ref[slot, 1:2, cs], jnp.max(lg, axis=0, keepdims=True))

        def member(parts, m):
            r = (m % 16) * 8
            return parts[m // 16][r:r + 8, :]

        planes = _bit_transpose32([member(keys, m) for m in range(WORD)])
        for i in range(WORD):
            planes_ref[slot, i, word_rows(t), :] = planes[i]
        valid_ref[slot, word_rows(t), :] = _tree_or(
            [jnp.where(member(valid, m), jnp.int32(_word_bit(m)), jnp.int32(0)) for m in range(WORD)])

    for_each_tile(score_tile)


def _dsa_attend(j, slot, lg0, avt_ref, o_ref, lg_ref, sel_ref, acc_ref, ml_ref):
    nq = DSA_QB
    n_tiles = j + 1
    tile_rows, lg_rows, word_rows, for_each_tile = _tile_helpers(n_tiles, lg0)
    queries = pl.ds(pl.multiple_of(j * nq, nq), nq)

    def masked_logits(t, cs, bias):
        return lg_ref[lg_rows(t), cs] + bias

    def mask_bias(t):
        selw = sel_ref[slot, word_rows(t), :]
        return jnp.concatenate(
            [jnp.where((selw << m) < 0, 0.0, NEG_BIG) for m in range(WORD)], axis=0)

    def attend(shift_row):
        acc_ref[...] = jnp.zeros(acc_ref.shape, F32)

        def attn_tile(t):
            vt = avt_ref[:, tile_rows(t)]
            bias = mask_bias(t)
            for h in range(A_HEADS):
                cs = slice(h * nq, (h + 1) * nq)
                p = jnp.exp2(masked_logits(t, cs, bias) - ml_ref[slot, shift_row:shift_row + 1, cs])
                acc_ref[:, cs] = acc_ref[:, cs] + _dot(vt, p.astype(BF16))

        for_each_tile(attn_tile)

    attend(1)
    underflow = jnp.min(acc_ref[A_HEAD_DIM:A_HEAD_DIM + 1, :]) < DENOM_MIN

    @pl.when(underflow)
    def _():
        ml_ref[slot, 0:1, :] = jnp.full((1, A_HEADS * nq), 0.5 * NEG_BIG, F32)

        def max_tile(t, carry):
            bias = mask_bias(t)
            for h in range(A_HEADS):
                cs = slice(h * nq, (h + 1) * nq)
                ml_ref[slot, 0:1, cs] = jnp.maximum(
                    ml_ref[slot, 0:1, cs], jnp.max(masked_logits(t, cs, bias), axis=0, keepdims=True))
            return carry

        lax.fori_loop(0, n_tiles, max_tile, 0)
        attend(0)

    out_t = acc_ref[0:A_HEAD_DIM, :] / acc_ref[A_HEAD_DIM:A_HEAD_DIM + 1, :]
    for hp in range(A_HEADS // 2):
        blk = jnp.concatenate(
            [out_t[:, (2 * hp) * nq:(2 * hp + 1) * nq], out_t[:, (2 * hp + 1) * nq:(2 * hp + 2) * nq]], axis=0)
        o_ref[queries, hp * 128:(hp + 1) * 128] = blk.T.astype(o_ref.dtype)


def _dsa(kk, avt, aqt, iqt, iwt, batch, seq, n_sel):
    t = batch * seq
    feat = lambda b: (0, b)
    return pl.pallas_call(
        functools.partial(_dsa_kernel, n_sel=n_sel, seq=seq),
        grid=(batch,),
        in_specs=[
            pl.BlockSpec((seq, 128), lambda b: (b, 0)),
            pl.BlockSpec((AV_ROWS, seq), feat),
            pl.BlockSpec((A_WIDTH, seq), feat),
            pl.BlockSpec((A_WIDTH, seq), feat),
            pl.BlockSpec((IDX_HEADS, seq), feat),
        ],
        out_specs=pl.BlockSpec((seq, A_WIDTH), lambda b: (b, 0)),
        out_shape=jax.ShapeDtypeStruct((t, A_WIDTH), BF16),
        scratch_shapes=[
            pltpu.VMEM((seq + KEY_TILE, A_HEADS * DSA_QB), F32),
            pltpu.VMEM((2, WORD, seq // WORD, DSA_QB), I32),
            pltpu.VMEM(((seq - 1).bit_length(), seq // WORD, DSA_QB), I32),
            pltpu.VMEM((2, seq // WORD, DSA_QB), I32),
            pltpu.VMEM((2, seq // WORD, DSA_QB), I32),
            pltpu.VMEM((128, A_HEADS * DSA_QB), BF16),
            pltpu.VMEM((128, A_HEADS * DSA_QB), BF16),
            pltpu.VMEM((AV_ROWS, A_HEADS * DSA_QB), F32),
            pltpu.VMEM((2, 8, A_HEADS * DSA_QB), F32),
        ],
        compiler_params=pltpu.CompilerParams(
            dimension_semantics=("arbitrary",), vmem_limit_bytes=VMEM_LIMIT),
        name="dsa",
    )(kk, avt, aqt, iqt, iwt)


def _log_gamma(h):
    return math.log1p(-(2.0 ** (-5.0 - h)))


def _retention_kernel(rq_ref, rkt_ref, rv_ref, rg_ref, gn_ref, o_ref,
                      state_ref, dmat_ref, dq_ref):
    nb = RET_BLOCK

    @pl.when(pl.program_id(0) == 0)
    def _():
        i = lax.broadcasted_iota(I32, (nb, nb), 0)
        jj = lax.broadcasted_iota(I32, (nb, nb), 1)
        dist = jnp.abs(i - jj).astype(F32)
        seen = (jj // CHUNK) <= (i // CHUNK)
        for h in range(R_HEADS):
            lg = _log_gamma(h)
            dmat_ref[h] = jnp.where(seen, jnp.exp(lg * dist), 0.0)
            dq_ref[h] = jnp.exp(lg * (i.astype(F32) + 1.0))

    state_ref[...] = jnp.zeros(state_ref.shape, F32)
    jrow = lax.broadcasted_iota(I32, (1, nb), 1).astype(F32)

    def block(n, carry):
        toks = pl.ds(pl.multiple_of(n * nb, nb), nb)
        for h in range(R_HEADS):
            lg = _log_gamma(h)
            q = rq_ref[toks, h * R_QK_DIM:(h + 1) * R_QK_DIM]
            kt = rkt_ref[h * R_QK_DIM:(h + 1) * R_QK_DIM, toks]
            v = rv_ref[toks, h * R_V_DIM:(h + 1) * R_V_DIM]
            state = state_ref[h]
            s = _dot(q, kt) * dmat_ref[h]
            y = _dot(s.astype(BF16), v) + _dot(q, state.astype(BF16)) * dq_ref[h]
            dk = jnp.exp(lg * (nb - 1.0 - jrow))
            kd = (kt.astype(F32) * dk).astype(BF16)
            state_ref[h] = state * math.exp(lg * nb) + _dot(kd, v)
            mean = jnp.mean(y, axis=-1, keepdims=True)
            yc = y - mean
            var = jnp.mean(yc * yc, axis=-1, keepdims=True)
            yn = yc * lax.rsqrt(var + GN_EPS) * gn_ref[:, h * R_V_DIM:(h + 1) * R_V_DIM]
            gate = rg_ref[toks, h * R_V_DIM:(h + 1) * R_V_DIM].astype(F32)
            o_ref[toks, h * R_V_DIM:(h + 1) * R_V_DIM] = (yn * (gate * jax.nn.sigmoid(gate))).astype(o_ref.dtype)
        return carry

    lax.fori_loop(0, rq_ref.shape[0] // nb, block, 0)


def _retention(rq, rkt, rv, rg, gn, batch, seq):
    t = batch * seq
    tok = lambda b: (b, 0)
    return pl.pallas_call(
        _retention_kernel,
        grid=(batch,),
        in_specs=[
            pl.BlockSpec((seq, 512), tok),
            pl.BlockSpec((512, seq), lambda b: (0, b)),
            pl.BlockSpec((seq, R_WIDTH), tok),
            pl.BlockSpec((seq, R_WIDTH), tok),
            pl.BlockSpec((1, R_WIDTH), lambda b: (0, 0)),
        ],
        out_specs=pl.BlockSpec((seq, R_WIDTH), tok),
        out_shape=jax.ShapeDtypeStruct((t, R_WIDTH), BF16),
        scratch_shapes=[
            pltpu.VMEM((R_HEADS, R_QK_DIM, R_V_DIM), F32),
            pltpu.VMEM((R_HEADS, RET_BLOCK, RET_BLOCK), F32),
            pltpu.VMEM((R_HEADS, RET_BLOCK, RET_BLOCK), F32),
        ],
        compiler_params=pltpu.CompilerParams(
            dimension_semantics=("arbitrary",), vmem_limit_bytes=VMEM_LIMIT),
        name="retention",
    )(rq, rkt, rv, rg, gn)


def _tail_kernel(h_ref, ya_ref, yr_ref, ga_ref, gb_ref, p_ref,
                 wa_ref, wb_ref, wo_ref, g2_ref, wg_ref, wu_ref, wd_ref,
                 gp_ref, wpg_ref, wpp_ref, gf_ref, o_ref):
    a = _dot(ya_ref[...], wa_ref[...])
    b = _dot(yr_ref[...], wb_ref[...])
    merged = (jax.nn.sigmoid(ga_ref[...].astype(F32)) * a
              + jax.nn.sigmoid(gb_ref[...].astype(F32)) * b)
    h = h_ref[...] + _dot(merged.astype(BF16), wo_ref[...])
    h = h + 0.5 * _swiglu(_rms(h, g2_ref[...]).astype(BF16), wg_ref, wu_ref, wd_ref)
    gate = jax.nn.sigmoid(_dot(_rms(h, gp_ref[...]).astype(BF16), wpg_ref[...]))
    emb = _dot(p_ref[...].astype(BF16), wpp_ref[...])
    o_ref[...] = _rms(h + gate * emb, gf_ref[...])


def _tail(h, ya, yr, ga, gb, p, wa, wb, wo, g2, wg, wu, wd, gp, wpg, wpp, gf, tm):
    t = h.shape[0]
    row = lambda i: (i, 0)
    return pl.pallas_call(
        _tail_kernel,
        grid=(t // tm,),
        in_specs=[
            pl.BlockSpec((tm, D_MODEL), row),
            pl.BlockSpec((tm, A_WIDTH), row),
            pl.BlockSpec((tm, R_WIDTH), row),
            pl.BlockSpec((tm, D_MODEL), row),
            pl.BlockSpec((tm, D_MODEL), row),
            pl.BlockSpec((tm, PLE_DIM), row),
            _const_spec((A_WIDTH, D_MODEL)),
            _const_spec((R_WIDTH, D_MODEL)),
            _const_spec((D_MODEL, D_MODEL)),
            _const_spec((1, D_MODEL)),
            _const_spec((D_MODEL, D_FF)),
            _const_spec((D_MODEL, D_FF)),
            _const_spec((D_FF, D_MODEL)),
            _const_spec((1, D_MODEL)),
            _const_spec((D_MODEL, D_MODEL)),
            _const_spec((PLE_DIM, D_MODEL)),
            _const_spec((1, D_MODEL)),
        ],
        out_specs=pl.BlockSpec((tm, D_MODEL), row),
        out_shape=jax.ShapeDtypeStruct((t, D_MODEL), F32),
        compiler_params=pltpu.CompilerParams(
            dimension_semantics=("parallel",), vmem_limit_bytes=VMEM_LIMIT),
        name="tail",
    )(h, ya, yr, ga, gb, p, wa, wb, wo, g2, wg, wu, wd, gp, wpg, wpp, gf)


def _token_tile(t):
    return 512 if t % 512 == 0 else 256


def kernel(x, p, positions, ffn1_norm, ffn1_w_gate, ffn1_w_up, ffn1_w_down, mix_norm, w_in, ret_gn, w_branch_a, w_branch_b, w_out, ffn2_norm, ffn2_w_gate, ffn2_w_up, ffn2_w_down, ple_norm, w_ple_gate, w_ple_proj, final_norm):
    batch, seq, _ = x.shape
    depth = p.shape[0]
    t = batch * seq
    tm = _token_tile(t)
    n_sel = min(TOPK_MAX, seq // 4)
    assert depth == 1, "the final norm is fused into the per-layer embedding step"
    assert seq % RET_BLOCK == 0 and seq % KEY_TILE == 0 and seq % DSA_QB == 0
    assert n_sel <= KEY_TILE and DSA_QB == KEY_TILE and KEY_TILE == 8 * WORD
    assert (seq // DSA_QB) % 2 == 0, "query blocks are processed in pairs"

    h = x.reshape(t, D_MODEL)
    pos3 = positions.reshape(t // tm, 1, tm)
    vec = lambda g: g.reshape(1, -1).astype(F32)

    for i in range(depth):
        w = w_in[i].astype(BF16)
        wt = jnp.concatenate([
            w[:, _OFF_AQ:_OFF_AQ + A_WIDTH], w[:, _OFF_IQ:_OFF_IQ + IDX_HEADS * IDX_DIM],
            w[:, _OFF_RQ:_OFF_RQ + 512], w[:, _OFF_RK:_OFF_RK + 512],
            w[:, _OFF_AK:_OFF_AK + 64], w[:, _OFF_IK:_OFF_IK + 64],
            w[:, _OFF_AV:_OFF_AV + 64], w[:, _OFF_IW:_OFF_IW + IDX_HEADS]], axis=1).T
        wn = w[:, _OFF_RV:]

        h = _ffn(h, vec(ffn1_norm[i]), ffn1_w_gate[i].astype(BF16), ffn1_w_up[i].astype(BF16),
                 ffn1_w_down[i].astype(BF16), tm)
        aqt, iqt, rq, rkt, kk, avt, iwt, rv, rg, ga, gb = _inproj(h, vec(mix_norm[i]), pos3, wt, wn, tm)
        ya = _dsa(kk, avt, aqt, iqt, iwt, batch, seq, n_sel)
        yr = _retention(rq, rkt, rv, rg, vec(ret_gn[i]), batch, seq)
        h = _tail(h, ya, yr, ga, gb, p[i].reshape(t, PLE_DIM),
                  w_branch_a[i].astype(BF16), w_branch_b[i].astype(BF16), w_out[i].astype(BF16),
                  vec(ffn2_norm[i]), ffn2_w_gate[i].astype(BF16), ffn2_w_up[i].astype(BF16),
                  ffn2_w_down[i].astype(BF16),
                  vec(ple_norm[i]), w_ple_gate[i].astype(BF16), w_ple_proj[i].astype(BF16),
                  vec(final_norm), tm)
    return h.reshape(batch, seq, D_MODEL)
```

```python
import functools
import math

import jax
import jax.numpy as jnp
import numpy as np
from jax import lax
from jax.experimental import pallas as pl
from jax.experimental.pallas import tpu as pltpu

F32 = jnp.float32
BF16 = jnp.bfloat16
I32 = jnp.int32

D_MODEL = 1024
CHUNK = 64
ROPE_THETA = 10000.0
EPS = 1e-6
GN_EPS = 1e-5
NEG_BIG = -1e30
A_HEADS = 8
A_HEAD_DIM = 64
IDX_HEADS = 8
IDX_DIM = 64
TOPK_MAX = 256
A_WIDTH = A_HEADS * A_HEAD_DIM
R_HEADS = 4
R_QK_DIM = 128
R_V_DIM = 256
R_WIDTH = R_HEADS * R_V_DIM
D_FF = 2816
PLE_DIM = 256

_OFF_AQ = 0
_OFF_AK = _OFF_AQ + A_WIDTH
_OFF_AV = _OFF_AK + A_HEAD_DIM
_OFF_IQ = _OFF_AV + A_HEAD_DIM
_OFF_IK = _OFF_IQ + IDX_HEADS * IDX_DIM
_OFF_IW = _OFF_IK + IDX_DIM
_OFF_RQ = _OFF_IW + IDX_HEADS
_OFF_RK = _OFF_RQ + R_HEADS * R_QK_DIM
_OFF_RV = _OFF_RK + R_HEADS * R_QK_DIM
_N_TOKEN_MAJOR = 2 * R_WIDTH + 2 * D_MODEL

_T_AQ = 0
_T_IQ = _T_AQ + A_WIDTH
_T_RQ = _T_IQ + IDX_HEADS * IDX_DIM
_T_RK = _T_RQ + R_HEADS * R_QK_DIM
_T_AK = _T_RK + R_HEADS * R_QK_DIM
_T_IK = _T_AK + A_HEAD_DIM
_T_AV = _T_IK + IDX_DIM
_T_IW = _T_AV + A_HEAD_DIM
_T_ROWS = _T_IW + IDX_HEADS

VMEM_LIMIT = 56 * 1024 * 1024
FF_CHUNK = 256
DSA_QB = 256
KEY_TILE = 256
KEY_HALF = KEY_TILE // 2
TILE_UNROLL = 4
RET_BLOCK = 256

AV_ROWS = A_HEAD_DIM + 16
INT_MIN = -(2 ** 31)
F32_MAX = 3.0e38
WORD = 32
LOG2_E = math.log2(math.e)
DENOM_MIN = 2.0 ** -100


def _const_spec(shape):
    nd = len(shape)
    return pl.BlockSpec(shape, lambda *_: (0,) * nd, pipeline_mode=pl.Buffered(1))


def _rms(x, g):
    ms = jnp.mean(x * x, axis=-1, keepdims=True)
    return x * lax.rsqrt(ms + EPS) * g


def _dot(a, b):
    return jnp.dot(a, b, preferred_element_type=F32)


def _swiglu(u, wg_ref, wu_ref, wd_ref):
    acc = jnp.zeros((u.shape[0], D_MODEL), F32)
    for c in range(D_FF // FF_CHUNK):
        sl = slice(c * FF_CHUNK, (c + 1) * FF_CHUNK)
        gate = _dot(u, wg_ref[:, sl])
        up = _dot(u, wu_ref[:, sl])
        act = (gate * jax.nn.sigmoid(gate) * up).astype(BF16)
        acc = acc + _dot(act, wd_ref[sl, :])
    return acc


def _ffn_kernel(h_ref, g_ref, wg_ref, wu_ref, wd_ref, o_ref):
    h = h_ref[...]
    o_ref[...] = h + 0.5 * _swiglu(_rms(h, g_ref[...]).astype(BF16), wg_ref, wu_ref, wd_ref)


def _ffn(h, g, wg, wu, wd, tm):
    t = h.shape[0]
    return pl.pallas_call(
        _ffn_kernel,
        grid=(t // tm,),
        in_specs=[
            pl.BlockSpec((tm, D_MODEL), lambda i: (i, 0)),
            _const_spec((1, D_MODEL)),
            _const_spec((D_MODEL, D_FF)),
            _const_spec((D_MODEL, D_FF)),
            _const_spec((D_FF, D_MODEL)),
        ],
        out_specs=pl.BlockSpec((tm, D_MODEL), lambda i: (i, 0)),
        out_shape=jax.ShapeDtypeStruct((t, D_MODEL), F32),
        compiler_params=pltpu.CompilerParams(
            dimension_semantics=("parallel",), vmem_limit_bytes=VMEM_LIMIT),
        name="ffn",
    )(h, g, wg, wu, wd)


def _rope_rows(t1, t2, cos, sin):
    return t1 * cos - t2 * sin, t2 * cos + t1 * sin


def _inproj_kernel(h_ref, g_ref, pos_ref, wt_ref, wn_ref,
                   aqt_ref, iqt_ref, rq_ref, rkt_ref, kk_ref, avt_ref, iwt_ref,
                   rv_ref, rg_ref, ga_ref, gb_ref):
    tm = h_ref.shape[0]
    u = _rms(h_ref[...], g_ref[...]).astype(BF16)

    zn = _dot(u, wn_ref[...])
    rv_ref[...] = zn[:, 0:R_WIDTH].astype(BF16)
    rg_ref[...] = zn[:, R_WIDTH:2 * R_WIDTH].astype(BF16)
    ga_ref[...] = zn[:, 2 * R_WIDTH:2 * R_WIDTH + D_MODEL].astype(BF16)
    gb_ref[...] = zn[:, 2 * R_WIDTH + D_MODEL:].astype(BF16)

    zt = lax.dot_general(wt_ref[...], u, (((1,), (1,)), ((), ())),
                         preferred_element_type=F32)

    pos = pos_ref[0].astype(F32)

    def angles(half):
        i = lax.broadcasted_iota(I32, (half, tm), 0).astype(F32)
        inv = jnp.exp(i * (-math.log(ROPE_THETA) / half))
        ang = pos * inv
        return jnp.cos(ang), jnp.sin(ang)

    cos64, sin64 = angles(A_HEAD_DIM // 2)
    cos128, sin128 = angles(R_QK_DIM // 2)

    def rope_head(row0, dim, cos, sin):
        half = dim // 2
        return _rope_rows(zt[row0:row0 + half], zt[row0 + half:row0 + dim], cos, sin)

    for h in range(A_HEADS):
        o1, o2 = rope_head(_T_AQ + h * A_HEAD_DIM, A_HEAD_DIM, cos64, sin64)
        r = h * A_HEAD_DIM
        aqt_ref[r:r + 32, :] = (o1 * (A_HEAD_DIM ** -0.5 * LOG2_E)).astype(BF16)
        aqt_ref[r + 32:r + 64, :] = (o2 * (A_HEAD_DIM ** -0.5 * LOG2_E)).astype(BF16)
    for h in range(IDX_HEADS):
        o1, o2 = rope_head(_T_IQ + h * IDX_DIM, IDX_DIM, cos64, sin64)
        r = h * IDX_DIM
        iqt_ref[r:r + 32, :] = (o1 * (IDX_DIM ** -0.5)).astype(BF16)
        iqt_ref[r + 32:r + 64, :] = (o2 * (IDX_DIM ** -0.5)).astype(BF16)

    for h in range(R_HEADS):
        o1, o2 = rope_head(_T_RQ + h * R_QK_DIM, R_QK_DIM, cos128, sin128)
        q = jnp.concatenate([o1, o2], axis=0)
        rq_ref[:, h * R_QK_DIM:(h + 1) * R_QK_DIM] = q.T.astype(BF16)
        k1, k2 = rope_head(_T_RK + h * R_QK_DIM, R_QK_DIM, cos128, sin128)
        r = h * R_QK_DIM
        rkt_ref[r:r + 64, :] = (k1 * (R_QK_DIM ** -0.5)).astype(BF16)
        rkt_ref[r + 64:r + 128, :] = (k2 * (R_QK_DIM ** -0.5)).astype(BF16)

    a1, a2 = rope_head(_T_AK, A_HEAD_DIM, cos64, sin64)
    i1, i2 = rope_head(_T_IK, IDX_DIM, cos64, sin64)
    kk = jnp.concatenate([a1, a2, i1, i2], axis=0)
    kk_ref[...] = kk.T.astype(BF16)

    avt_ref[0:A_HEAD_DIM, :] = zt[_T_AV:_T_AV + A_HEAD_DIM].astype(BF16)
    avt_ref[A_HEAD_DIM:AV_ROWS, :] = jnp.ones((AV_ROWS - A_HEAD_DIM, tm), BF16)
    iwt_ref[...] = zt[_T_IW:_T_IW + IDX_HEADS] * (IDX_HEADS ** -0.5)


def _inproj(h, g, pos3, wt, wn, tm):
    t = h.shape[0]
    nt = t // tm
    row = lambda i: (i, 0)
    col = lambda i: (0, i)
    outs = [
        (jax.ShapeDtypeStruct((A_WIDTH, t), BF16), pl.BlockSpec((A_WIDTH, tm), col)),
        (jax.ShapeDtypeStruct((A_WIDTH, t), BF16), pl.BlockSpec((A_WIDTH, tm), col)),
        (jax.ShapeDtypeStruct((t, 512), BF16), pl.BlockSpec((tm, 512), row)),
        (jax.ShapeDtypeStruct((512, t), BF16), pl.BlockSpec((512, tm), col)),
        (jax.ShapeDtypeStruct((t, 128), BF16), pl.BlockSpec((tm, 128), row)),
        (jax.ShapeDtypeStruct((AV_ROWS, t), BF16), pl.BlockSpec((AV_ROWS, tm), col)),
        (jax.ShapeDtypeStruct((IDX_HEADS, t), F32), pl.BlockSpec((IDX_HEADS, tm), col)),
        (jax.ShapeDtypeStruct((t, R_WIDTH), BF16), pl.BlockSpec((tm, R_WIDTH), row)),
        (jax.ShapeDtypeStruct((t, R_WIDTH), BF16), pl.BlockSpec((tm, R_WIDTH), row)),
        (jax.ShapeDtypeStruct((t, D_MODEL), BF16), pl.BlockSpec((tm, D_MODEL), row)),
        (jax.ShapeDtypeStruct((t, D_MODEL), BF16), pl.BlockSpec((tm, D_MODEL), row)),
    ]
    return pl.pallas_call(
        _inproj_kernel,
        grid=(nt,),
        in_specs=[
            pl.BlockSpec((tm, D_MODEL), row),
            _const_spec((1, D_MODEL)),
            pl.BlockSpec((1, 1, tm), lambda i: (i, 0, 0)),
            _const_spec((_T_ROWS, D_MODEL)),
            _const_spec((D_MODEL, _N_TOKEN_MAJOR)),
        ],
        out_specs=[o[1] for o in outs],
        out_shape=[o[0] for o in outs],
        compiler_params=pltpu.CompilerParams(
            dimension_semantics=("parallel",), vmem_limit_bytes=VMEM_LIMIT),
        name="inproj",
    )(h, g, pos3, wt, wn)


_BT_MASKS = {16: 0x0000FFFF, 8: 0x00FF00FF, 4: 0x0F0F0F0F, 2: 0x33333333, 1: 0x55555555}


def _bit_transpose32(rows):
    a = list(rows)
    j = 16
    while j:
        mask = _BT_MASKS[j]
        k = 0
        while k < WORD:
            t = (a[k] ^ (a[k + j] >> j)) & mask
            a[k] = a[k] ^ t
            a[k + j] = a[k + j] ^ (t << j)
            k = (k + j + 1) & ~j
        j >>= 1
    return a


def _dsa_kernel(kk_ref, avt_ref, aqt_ref, iqt_ref, iwt_ref, o_ref,
                lg_ref, planes_ref, posp_ref, sc_ref, thr_ref, pmax_ref, qa_ref, qi_ref,
                acc_ref, ml_ref, *, n_sel, seq):
    nq = DSA_QB
    words = seq // WORD
    wpt = KEY_TILE // WORD
    pos_bits = (seq - 1).bit_length()

    @pl.when(pl.program_id(0) == 0)
    def _():
        qa_ref[64:128, :] = jnp.zeros((64, A_HEADS * nq), BF16)
        qi_ref[0:64, :] = jnp.zeros((64, A_HEADS * nq), BF16)
        planes_ref[...] = jnp.zeros(planes_ref.shape, I32)
        for t in range(seq // KEY_TILE):
            sub = lax.broadcasted_iota(I32, (wpt, nq), 0)
            p = _bit_transpose32([(seq - 1 - t * KEY_TILE - m * wpt) - sub for m in range(WORD)])
            for i in range(pos_bits):
                posp_ref[i, t * wpt:(t + 1) * wpt, :] = p[WORD - pos_bits + i]

    def query_block(j, carry):
        _dsa_scores(j, kk_ref, aqt_ref, iqt_ref, iwt_ref, lg_ref, planes_ref, sc_ref, qa_ref, qi_ref, ml_ref)

        wrow = lax.broadcasted_iota(I32, (words, nq), 0)
        zero = jnp.zeros((1, nq), I32)
        alive = jnp.where(wrow < (j + 1) * wpt, jnp.int32(-1), jnp.int32(0))
        alive, above, key = lax.fori_loop(
            0, WORD, lambda i, s: _radix_step(planes_ref[i], s, n_sel), (alive, zero, zero))
        _, _, rev = lax.fori_loop(
            0, pos_bits, lambda i, s: _radix_step(posp_ref[i], s, n_sel), (alive, above, zero))

        _confirm_threshold(j + 1, _key_to_score(key), (seq - 1) - rev, sc_ref, thr_ref, pmax_ref, n_sel, seq)
        _dsa_attend(j, avt_ref, o_ref, lg_ref, sc_ref, thr_ref, pmax_ref, acc_ref, ml_ref)
        return carry

    lax.fori_loop(0, seq // nq, query_block, 0)


def _radix_step(plane, state, n_sel):
    alive, above, bits = state
    ones = alive & plane
    c = jnp.sum(lax.population_count(ones), axis=0, keepdims=True)
    keep_ones = (above + c) >= n_sel
    return (jnp.where(keep_ones, ones, alive ^ ones),
            jnp.where(keep_ones, above, above + c),
            (bits << 1) | jnp.where(keep_ones, 1, 0))


def _key_to_score(key):
    bits = jnp.where(key < 0, key ^ jnp.int32(INT_MIN), ~key)
    return pltpu.bitcast(bits, F32)


def _confirm_threshold(n_tiles, thr0, pmax0, sc_ref, thr_ref, pmax_ref, n_sel, seq):
    nq = thr0.shape[1]
    tile_rows, _, _ = _tile_helpers(n_tiles)
    k = float(n_sel)

    def fold(x, op):
        return op(x.reshape(KEY_TILE // 8, 8, nq), axis=0)

    def survey(thr):
        def tile(t, acc):
            n_gt, n_ge, n_tie, above, below = acc
            s = sc_ref[tile_rows(t), :]
            pos = t * KEY_TILE + lax.broadcasted_iota(I32, (KEY_TILE, nq), 0)
            gt, ge = s > thr, s >= thr
            return (n_gt + fold(jnp.where(gt, 1.0, 0.0), jnp.sum),
                    n_ge + fold(jnp.where(ge, 1.0, 0.0), jnp.sum),
                    n_tie + fold(jnp.where((s == thr) & (pos <= pmax0), 1.0, 0.0), jnp.sum),
                    jnp.minimum(above, fold(jnp.where(gt, s, F32_MAX), jnp.min)),
                    jnp.maximum(below, fold(jnp.where(ge, -F32_MAX, s), jnp.max)))
        zero8 = jnp.zeros((8, nq), F32)
        acc = (zero8, zero8, zero8, jnp.full((8, nq), F32_MAX, F32), jnp.full((8, nq), -F32_MAX, F32))
        n_gt, n_ge, n_tie, above, below = lax.fori_loop(0, n_tiles, tile, acc)
        total = lambda x: jnp.sum(x, axis=0, keepdims=True)
        return (total(n_gt), total(n_ge), total(n_tie),
                jnp.min(above, axis=0, keepdims=True), jnp.max(below, axis=0, keepdims=True))

    def step(state):
        thr = state[0]
        n_gt, n_ge, n_tie, above, below = survey(thr)
        too_low, too_high = n_gt >= k, n_ge < k
        off = jnp.max(jnp.where(too_low | too_high, 1.0, 0.0)) > 0.0
        return jnp.where(too_low, above, jnp.where(too_high, below, thr)), n_gt, n_ge, n_tie, off

    zero = jnp.zeros((1, nq), F32)
    thr, n_gt, n_ge, n_tie, _ = lax.while_loop(lambda s: s[4], step, (thr0, zero, zero, zero, jnp.bool_(True)))

    want = k - n_gt
    take_all = (n_ge - n_gt) == want
    live = thr > 0.5 * NEG_BIG
    redo = live & jnp.logical_not(take_all) & (n_tie != want)
    thr_ref[0:1, :] = thr
    thr_ref[1:2, :] = jnp.maximum(thr, 0.5 * NEG_BIG)
    pmax_ref[0:1, :] = jnp.where(live, jnp.where(take_all, jnp.int32(seq), pmax0), jnp.int32(-1))

    @pl.when(jnp.max(jnp.where(redo, 1.0, 0.0)) > 0.0)
    def _():
        def count_before(bound):
            def tile(t, acc):
                s = sc_ref[tile_rows(t), :]
                pos = t * KEY_TILE + lax.broadcasted_iota(I32, (KEY_TILE, nq), 0)
                return acc + fold(jnp.where((s == thr) & (pos < bound), 1.0, 0.0), jnp.sum)
            return jnp.sum(lax.fori_loop(0, n_tiles, tile, jnp.zeros((8, nq), F32)), axis=0, keepdims=True)

        def bisect(i, p):
            cand = p + jnp.left_shift(jnp.int32(1), (seq - 1).bit_length() - 1 - i)
            return jnp.where(count_before(cand) < want, cand, p)

        p = lax.fori_loop(0, (seq - 1).bit_length(), bisect, jnp.zeros((1, nq), I32))
        pmax_ref[0:1, :] = jnp.where(redo, p, pmax_ref[0:1, :])


def _tile_helpers(n_tiles):
    wpt = KEY_TILE // WORD

    def tile_rows(t):
        return pl.ds(pl.multiple_of(t * KEY_TILE, KEY_TILE), KEY_TILE)

    def word_rows(t):
        return pl.ds(pl.multiple_of(t * wpt, wpt), wpt)

    def for_each_tile(body):
        def group(i, carry):
            for u in range(TILE_UNROLL):
                body(TILE_UNROLL * i + u)
            return carry

        lax.fori_loop(0, n_tiles // TILE_UNROLL, group, 0)
        done = n_tiles - n_tiles % TILE_UNROLL
        width = TILE_UNROLL // 2
        while width:
            @pl.when((n_tiles // width) % 2 == 1)
            def _(done=done, width=width):
                for u in range(width):
                    body(done + u)
            done = done + (n_tiles // width) % 2 * width
            width //= 2

    return tile_rows, word_rows, for_each_tile


def _dsa_scores(j, kk_ref, aqt_ref, iqt_ref, iwt_ref, lg_ref, planes_ref, sc_ref, qa_ref, qi_ref, ml_ref):
    nq = DSA_QB
    tile_rows, word_rows, for_each_tile = _tile_helpers(j + 1)
    queries = pl.ds(pl.multiple_of(j * nq, nq), nq)

    for h in range(A_HEADS):
        cs = slice(h * nq, (h + 1) * nq)
        qa_ref[0:64, cs] = aqt_ref[h * 64:(h + 1) * 64, queries]
        qi_ref[64:128, cs] = iqt_ref[h * 64:(h + 1) * 64, queries]

    lane = lax.broadcasted_iota(I32, (1, nq), 1)
    vis_end = (j * nq + (lane // CHUNK + 1) * CHUNK)
    w = iwt_ref[:, queries]
    ml_ref[1:2, :] = jnp.full((1, A_HEADS * nq), 0.5 * NEG_BIG, F32)

    def score_tile(t):
        keys = []
        for half in range(2):
            r0 = pl.multiple_of(t * KEY_TILE + half * KEY_HALF, KEY_HALF)
            kt = kk_ref[pl.ds(r0, KEY_HALF), :]
            sc = None
            for h in range(IDX_HEADS):
                rel = jnp.maximum(_dot(kt, qi_ref[:, h * nq:(h + 1) * nq]), 0.0)
                part = rel * w[h:h + 1, :]
                sc = part if sc is None else sc + part
            sc = jnp.where(sc == 0.0, 0.0, sc)
            spos = r0 + lax.broadcasted_iota(I32, (KEY_HALF, nq), 0)
            sc = jnp.where(spos < vis_end, sc, NEG_BIG)
            sc_ref[pl.ds(r0, KEY_HALF), :] = sc
            bits = pltpu.bitcast(sc, I32)
            keys.append(bits ^ ((bits >> 31) | jnp.int32(INT_MIN)))

        kt_all = kk_ref[tile_rows(t), :]
        for h in range(A_HEADS):
            cs = slice(h * nq, (h + 1) * nq)
            lg = _dot(kt_all, qa_ref[:, cs])
            lg_ref[tile_rows(t), cs] = lg
            ml_ref[1:2, cs] = jnp.maximum(ml_ref[1:2, cs], jnp.max(lg, axis=0, keepdims=True))

        def member(parts, m):
            r = (m % 16) * 8
            return parts[m // 16][r:r + 8, :]

        planes = _bit_transpose32([member(keys, m) for m in range(WORD)])
        for i in range(WORD):
            planes_ref[i, word_rows(t), :] = planes[i]

    for_each_tile(score_tile)


def _dsa_attend(j, avt_ref, o_ref, lg_ref, sc_ref, thr_ref, pmax_ref, acc_ref, ml_ref):
    nq = DSA_QB
    n_tiles = j + 1
    tile_rows, word_rows, for_each_tile = _tile_helpers(n_tiles)
    queries = pl.ds(pl.multiple_of(j * nq, nq), nq)

    def masked_logits(t, cs, bias):
        return lg_ref[tile_rows(t), cs] + bias

    thr, thr_valid, pmax = thr_ref[0:1, :], thr_ref[1:2, :], pmax_ref[0:1, :]

    def mask_bias(t):
        s = sc_ref[tile_rows(t), :]
        pos = t * KEY_TILE + lax.broadcasted_iota(I32, (KEY_TILE, nq), 0)
        selected = (s > thr_valid) | ((s == thr) & (pos <= pmax))
        return jnp.where(selected, 0.0, NEG_BIG)

    def attend(shift_row):
        acc_ref[...] = jnp.zeros(acc_ref.shape, F32)

        def attn_tile(t):
            vt = avt_ref[:, tile_rows(t)]
            bias = mask_bias(t)
            for h in range(A_HEADS):
                cs = slice(h * nq, (h + 1) * nq)
                p = jnp.exp2(masked_logits(t, cs, bias) - ml_ref[shift_row:shift_row + 1, cs])
                acc_ref[:, cs] = acc_ref[:, cs] + _dot(vt, p.astype(BF16))

        for_each_tile(attn_tile)

    attend(1)
    underflow = jnp.min(acc_ref[A_HEAD_DIM:A_HEAD_DIM + 1, :]) < DENOM_MIN

    @pl.when(underflow)
    def _():
        ml_ref[0:1, :] = jnp.full((1, A_HEADS * nq), 0.5 * NEG_BIG, F32)

        def max_tile(t, carry):
            bias = mask_bias(t)
            for h in range(A_HEADS):
                cs = slice(h * nq, (h + 1) * nq)
                ml_ref[0:1, cs] = jnp.maximum(
                    ml_ref[0:1, cs], jnp.max(masked_logits(t, cs, bias), axis=0, keepdims=True))
            return carry

        lax.fori_loop(0, n_tiles, max_tile, 0)
        attend(0)

    out_t = acc_ref[0:A_HEAD_DIM, :] / acc_ref[A_HEAD_DIM:A_HEAD_DIM + 1, :]
    for hp in range(A_HEADS // 2):
        blk = jnp.concatenate(
            [out_t[:, (2 * hp) * nq:(2 * hp + 1) * nq], out_t[:, (2 * hp + 1) * nq:(2 * hp + 2) * nq]], axis=0)
        o_ref[queries, hp * 128:(hp + 1) * 128] = blk.T.astype(o_ref.dtype)


def _dsa(kk, avt, aqt, iqt, iwt, batch, seq, n_sel):
    t = batch * seq
    feat = lambda b: (0, b)
    return pl.pallas_call(
        functools.partial(_dsa_kernel, n_sel=n_sel, seq=seq),
        grid=(batch,),
        in_specs=[
            pl.BlockSpec((seq, 128), lambda b: (b, 0)),
            pl.BlockSpec((AV_ROWS, seq), feat),
            pl.BlockSpec((A_WIDTH, seq), feat),
            pl.BlockSpec((A_WIDTH, seq), feat),
            pl.BlockSpec((IDX_HEADS, seq), feat),
        ],
        out_specs=pl.BlockSpec((seq, A_WIDTH), lambda b: (b, 0)),
        out_shape=jax.ShapeDtypeStruct((t, A_WIDTH), BF16),
        scratch_shapes=[
            pltpu.VMEM((seq, A_HEADS * DSA_QB), F32),
            pltpu.VMEM((WORD, seq // WORD, DSA_QB), I32),
            pltpu.VMEM(((seq - 1).bit_length(), seq // WORD, DSA_QB), I32),
            pltpu.VMEM((seq, DSA_QB), F32),
            pltpu.VMEM((8, DSA_QB), F32),
            pltpu.VMEM((8, DSA_QB), I32),
            pltpu.VMEM((128, A_HEADS * DSA_QB), BF16),
            pltpu.VMEM((128, A_HEADS * DSA_QB), BF16),
            pltpu.VMEM((AV_ROWS, A_HEADS * DSA_QB), F32),
            pltpu.VMEM((8, A_HEADS * DSA_QB), F32),
        ],
        compiler_params=pltpu.CompilerParams(
            dimension_semantics=("arbitrary",), vmem_limit_bytes=VMEM_LIMIT),
        name="dsa",
    )(kk, avt, aqt, iqt, iwt)


def _log_gamma(h):
    return math.log1p(-(2.0 ** (-5.0 - h)))


def _retention_kernel(rq_ref, rkt_ref, rv_ref, rg_ref, gn_ref, o_ref,
                      state_ref, dmat_ref, dq_ref):
    nb = RET_BLOCK

    @pl.when(pl.program_id(0) == 0)
    def _():
        i = lax.broadcasted_iota(I32, (nb, nb), 0)
        jj = lax.broadcasted_iota(I32, (nb, nb), 1)
        dist = jnp.abs(i - jj).astype(F32)
        seen = (jj // CHUNK) <= (i // CHUNK)
        for h in range(R_HEADS):
            lg = _log_gamma(h)
            dmat_ref[h] = jnp.where(seen, jnp.exp(lg * dist), 0.0)
            dq_ref[h] = jnp.exp(lg * (i.astype(F32) + 1.0))

    state_ref[...] = jnp.zeros(state_ref.shape, F32)
    jrow = lax.broadcasted_iota(I32, (1, nb), 1).astype(F32)

    def block(n, carry):
        toks = pl.ds(pl.multiple_of(n * nb, nb), nb)
        for h in range(R_HEADS):
            lg = _log_gamma(h)
            q = rq_ref[toks, h * R_QK_DIM:(h + 1) * R_QK_DIM]
            kt = rkt_ref[h * R_QK_DIM:(h + 1) * R_QK_DIM, toks]
            v = rv_ref[toks, h * R_V_DIM:(h + 1) * R_V_DIM]
            state = state_ref[h]
            s = _dot(q, kt) * dmat_ref[h]
            y = _dot(s.astype(BF16), v) + _dot(q, state.astype(BF16)) * dq_ref[h]
            dk = jnp.exp(lg * (nb - 1.0 - jrow))
            kd = (kt.astype(F32) * dk).astype(BF16)
            state_ref[h] = state * math.exp(lg * nb) + _dot(kd, v)
            mean = jnp.mean(y, axis=-1, keepdims=True)
            yc = y - mean
            var = jnp.mean(yc * yc, axis=-1, keepdims=True)
            yn = yc * lax.rsqrt(var + GN_EPS) * gn_ref[:, h * R_V_DIM:(h + 1) * R_V_DIM]
            gate = rg_ref[toks, h * R_V_DIM:(h + 1) * R_V_DIM].astype(F32)
            o_ref[toks, h * R_V_DIM:(h + 1) * R_V_DIM] = (yn * (gate * jax.nn.sigmoid(gate))).astype(o_ref.dtype)
        return carry

    lax.fori_loop(0, rq_ref.shape[0] // nb, block, 0)


def _retention(rq, rkt, rv, rg, gn, batch, seq):
    t = batch * seq
    tok = lambda b: (b, 0)
    return pl.pallas_call(
        _retention_kernel,
        grid=(batch,),
        in_specs=[
            pl.BlockSpec((seq, 512), tok),
            pl.BlockSpec((512, seq), lambda b: (0, b)),
            pl.BlockSpec((seq, R_WIDTH), tok),
            pl.BlockSpec((seq, R_WIDTH), tok),
            pl.BlockSpec((1, R_WIDTH), lambda b: (0, 0)),
        ],
        out_specs=pl.BlockSpec((seq, R_WIDTH), tok),
        out_shape=jax.ShapeDtypeStruct((t, R_WIDTH), BF16),
        scratch_shapes=[
            pltpu.VMEM((R_HEADS, R_QK_DIM, R_V_DIM), F32),
            pltpu.VMEM((R_HEADS, RET_BLOCK, RET_BLOCK), F32),
            pltpu.VMEM((R_HEADS, RET_BLOCK, RET_BLOCK), F32),
        ],
        compiler_params=pltpu.CompilerParams(
            dimension_semantics=("arbitrary",), vmem_limit_bytes=VMEM_LIMIT),
        name="retention",
    )(rq, rkt, rv, rg, gn)


def _tail_kernel(h_ref, ya_ref, yr_ref, ga_ref, gb_ref, p_ref,
                 wa_ref, wb_ref, wo_ref, g2_ref, wg_ref, wu_ref, wd_ref,
                 gp_ref, wpg_ref, wpp_ref, gf_ref, o_ref):
    a = _dot(ya_ref[...], wa_ref[...])
    b = _dot(yr_ref[...], wb_ref[...])
    merged = (jax.nn.sigmoid(ga_ref[...].astype(F32)) * a
              + jax.nn.sigmoid(gb_ref[...].astype(F32)) * b)
    h = h_ref[...] + _dot(merged.astype(BF16), wo_ref[...])
    h = h + 0.5 * _swiglu(_rms(h, g2_ref[...]).astype(BF16), wg_ref, wu_ref, wd_ref)
    gate = jax.nn.sigmoid(_dot(_rms(h, gp_ref[...]).astype(BF16), wpg_ref[...]))
    emb = _dot(p_ref[...].astype(BF16), wpp_ref[...])
    o_ref[...] = _rms(h + gate * emb, gf_ref[...])


def _tail(h, ya, yr, ga, gb, p, wa, wb, wo, g2, wg, wu, wd, gp, wpg, wpp, gf, tm):
    t = h.shape[0]
    row = lambda i: (i, 0)
    return pl.pallas_call(
        _tail_kernel,
        grid=(t // tm,),
        in_specs=[
            pl.BlockSpec((tm, D_MODEL), row),
            pl.BlockSpec((tm, A_WIDTH), row),
            pl.BlockSpec((tm, R_WIDTH), row),
            pl.BlockSpec((tm, D_MODEL), row),
            pl.BlockSpec((tm, D_MODEL), row),
            pl.BlockSpec((tm, PLE_DIM), row),
            _const_spec((A_WIDTH, D_MODEL)),
            _const_spec((R_WIDTH, D_MODEL)),
            _const_spec((D_MODEL, D_MODEL)),
            _const_spec((1, D_MODEL)),
            _const_spec((D_MODEL, D_FF)),
            _const_spec((D_MODEL, D_FF)),
            _const_spec((D_FF, D_MODEL)),
            _const_spec((1, D_MODEL)),
            _const_spec((D_MODEL, D_MODEL)),
            _const_spec((PLE_DIM, D_MODEL)),
            _const_spec((1, D_MODEL)),
        ],
        out_specs=pl.BlockSpec((tm, D_MODEL), row),
        out_shape=jax.ShapeDtypeStruct((t, D_MODEL), F32),
        compiler_params=pltpu.CompilerParams(
            dimension_semantics=("parallel",), vmem_limit_bytes=VMEM_LIMIT),
        name="tail",
    )(h, ya, yr, ga, gb, p, wa, wb, wo, g2, wg, wu, wd, gp, wpg, wpp, gf)


def _token_tile(t):
    return 512 if t % 512 == 0 else 256


def kernel(x, p, positions, ffn1_norm, ffn1_w_gate, ffn1_w_up, ffn1_w_down, mix_norm, w_in, ret_gn, w_branch_a, w_branch_b, w_out, ffn2_norm, ffn2_w_gate, ffn2_w_up, ffn2_w_down, ple_norm, w_ple_gate, w_ple_proj, final_norm):
    batch, seq, _ = x.shape
    depth = p.shape[0]
    t = batch * seq
    tm = _token_tile(t)
    n_sel = min(TOPK_MAX, seq // 4)
    assert depth == 1, "the final norm is fused into the per-layer embedding step"
    assert seq % RET_BLOCK == 0 and seq % KEY_TILE == 0 and seq % DSA_QB == 0
    assert n_sel <= KEY_TILE and DSA_QB == KEY_TILE and KEY_TILE == 8 * WORD

    h = x.reshape(t, D_MODEL)
    pos3 = positions.reshape(t // tm, 1, tm)
    vec = lambda g: g.reshape(1, -1).astype(F32)

    for i in range(depth):
        w = w_in[i].astype(BF16)
        wt = jnp.concatenate([
            w[:, _OFF_AQ:_OFF_AQ + A_WIDTH], w[:, _OFF_IQ:_OFF_IQ + IDX_HEADS * IDX_DIM],
            w[:, _OFF_RQ:_OFF_RQ + 512], w[:, _OFF_RK:_OFF_RK + 512],
            w[:, _OFF_AK:_OFF_AK + 64], w[:, _OFF_IK:_OFF_IK + 64],
            w[:, _OFF_AV:_OFF_AV + 64], w[:, _OFF_IW:_OFF_IW + IDX_HEADS]], axis=1).T
        wn = w[:, _OFF_RV:]

        h = _ffn(h, vec(ffn1_norm[i]), ffn1_w_gate[i].astype(BF16), ffn1_w_up[i].astype(BF16),
                 ffn1_w_down[i].astype(BF16), tm)
        aqt, iqt, rq, rkt, kk, avt, iwt, rv, rg, ga, gb = _inproj(h, vec(mix_norm[i]), pos3, wt, wn, tm)
        ya = _dsa(kk, avt, aqt, iqt, iwt, batch, seq, n_sel)
        yr = _retention(rq, rkt, rv, rg, vec(ret_gn[i]), batch, seq)
        h = _tail(h, ya, yr, ga, gb, p[i].reshape(t, PLE_DIM),
                  w_branch_a[i].astype(BF16), w_branch_b[i].astype(BF16), w_out[i].astype(BF16),
                  vec(ffn2_norm[i]), ffn2_w_gate[i].astype(BF16), ffn2_w_up[i].astype(BF16),
                  ffn2_w_down[i].astype(BF16),
                  vec(ple_norm[i]), w_ple_gate[i].astype(BF16), w_ple_proj[i].astype(BF16),
                  vec(final_norm), tm)
    return h.reshape(batch, seq, D_MODEL)
```

```python
import functools
import math

import jax
import jax.numpy as jnp
import numpy as np
from jax import lax
from jax.experimental import pallas as pl
from jax.experimental.pallas import tpu as pltpu

F32 = jnp.float32
BF16 = jnp.bfloat16
I32 = jnp.int32

D_MODEL = 1024
CHUNK = 64
ROPE_THETA = 10000.0
EPS = 1e-6
GN_EPS = 1e-5
NEG_BIG = -1e30
A_HEADS = 8
A_HEAD_DIM = 64
IDX_HEADS = 8
IDX_DIM = 64
TOPK_MAX = 256
A_WIDTH = A_HEADS * A_HEAD_DIM
R_HEADS = 4
R_QK_DIM = 128
R_V_DIM = 256
R_WIDTH = R_HEADS * R_V_DIM
D_FF = 2816
PLE_DIM = 256

_OFF_AQ = 0
_OFF_AK = _OFF_AQ + A_WIDTH
_OFF_AV = _OFF_AK + A_HEAD_DIM
_OFF_IQ = _OFF_AV + A_HEAD_DIM
_OFF_IK = _OFF_IQ + IDX_HEADS * IDX_DIM
_OFF_IW = _OFF_IK + IDX_DIM
_OFF_RQ = _OFF_IW + IDX_HEADS
_OFF_RK = _OFF_RQ + R_HEADS * R_QK_DIM
_OFF_RV = _OFF_RK + R_HEADS * R_QK_DIM
_N_TOKEN_MAJOR = 2 * R_WIDTH + 2 * D_MODEL

_T_AQ = 0
_T_IQ = _T_AQ + A_WIDTH
_T_RQ = _T_IQ + IDX_HEADS * IDX_DIM
_T_RK = _T_RQ + R_HEADS * R_QK_DIM
_T_AK = _T_RK + R_HEADS * R_QK_DIM
_T_IK = _T_AK + A_HEAD_DIM
_T_AV = _T_IK + IDX_DIM
_T_IW = _T_AV + A_HEAD_DIM
_T_ROWS = _T_IW + IDX_HEADS

VMEM_LIMIT = 56 * 1024 * 1024
FF_CHUNK = 256
DSA_QB = 256
KEY_TILE = 256
KEY_HALF = KEY_TILE // 2
TILE_UNROLL = 4
RET_BLOCK = 256

AV_ROWS = A_HEAD_DIM + 16
INT_MIN = -(2 ** 31)
F32_MAX = 3.0e38
WORD = 32
LOG2_E = math.log2(math.e)
DENOM_MIN = 2.0 ** -100


def _const_spec(shape):
    nd = len(shape)
    return pl.BlockSpec(shape, lambda *_: (0,) * nd, pipeline_mode=pl.Buffered(1))


def _rms(x, g):
    ms = jnp.mean(x * x, axis=-1, keepdims=True)
    return x * lax.rsqrt(ms + EPS) * g


def _dot(a, b):
    return jnp.dot(a, b, preferred_element_type=F32)


def _swiglu(u, wg_ref, wu_ref, wd_ref):
    acc = jnp.zeros((u.shape[0], D_MODEL), F32)
    for c in range(D_FF // FF_CHUNK):
        sl = slice(c * FF_CHUNK, (c + 1) * FF_CHUNK)
        gate = _dot(u, wg_ref[:, sl])
        up = _dot(u, wu_ref[:, sl])
        act = (gate * jax.nn.sigmoid(gate) * up).astype(BF16)
        acc = acc + _dot(act, wd_ref[sl, :])
    return acc


def _ffn_kernel(h_ref, g_ref, wg_ref, wu_ref, wd_ref, o_ref):
    h = h_ref[...]
    o_ref[...] = h + 0.5 * _swiglu(_rms(h, g_ref[...]).astype(BF16), wg_ref, wu_ref, wd_ref)


def _ffn(h, g, wg, wu, wd, tm):
    t = h.shape[0]
    return pl.pallas_call(
        _ffn_kernel,
        grid=(t // tm,),
        in_specs=[
            pl.BlockSpec((tm, D_MODEL), lambda i: (i, 0)),
            _const_spec((1, D_MODEL)),
            _const_spec((D_MODEL, D_FF)),
            _const_spec((D_MODEL, D_FF)),
            _const_spec((D_FF, D_MODEL)),
        ],
        out_specs=pl.BlockSpec((tm, D_MODEL), lambda i: (i, 0)),
        out_shape=jax.ShapeDtypeStruct((t, D_MODEL), F32),
        compiler_params=pltpu.CompilerParams(
            dimension_semantics=("parallel",), vmem_limit_bytes=VMEM_LIMIT),
        name="ffn",
    )(h, g, wg, wu, wd)


def _rope_rows(t1, t2, cos, sin):
    return t1 * cos - t2 * sin, t2 * cos + t1 * sin


def _inproj_kernel(h_ref, g_ref, pos_ref, wt_ref, wn_ref,
                   aqt_ref, iqt_ref, rq_ref, rkt_ref, kk_ref, avt_ref, iwt_ref,
                   rv_ref, rg_ref, ga_ref, gb_ref):
    tm = h_ref.shape[0]
    u = _rms(h_ref[...], g_ref[...]).astype(BF16)

    zn = _dot(u, wn_ref[...])
    rv_ref[...] = zn[:, 0:R_WIDTH].astype(BF16)
    rg_ref[...] = zn[:, R_WIDTH:2 * R_WIDTH].astype(BF16)
    ga_ref[...] = zn[:, 2 * R_WIDTH:2 * R_WIDTH + D_MODEL].astype(BF16)
    gb_ref[...] = zn[:, 2 * R_WIDTH + D_MODEL:].astype(BF16)

    zt = lax.dot_general(wt_ref[...], u, (((1,), (1,)), ((), ())),
                         preferred_element_type=F32)

    pos = pos_ref[0].astype(F32)

    def angles(half):
        i = lax.broadcasted_iota(I32, (half, tm), 0).astype(F32)
        inv = jnp.exp(i * (-math.log(ROPE_THETA) / half))
        ang = pos * inv
        return jnp.cos(ang), jnp.sin(ang)

    cos64, sin64 = angles(A_HEAD_DIM // 2)
    cos128, sin128 = angles(R_QK_DIM // 2)

    def rope_head(row0, dim, cos, sin):
        half = dim // 2
        return _rope_rows(zt[row0:row0 + half], zt[row0 + half:row0 + dim], cos, sin)

    for h in range(A_HEADS):
        o1, o2 = rope_head(_T_AQ + h * A_HEAD_DIM, A_HEAD_DIM, cos64, sin64)
        r = h * A_HEAD_DIM
        aqt_ref[r:r + 32, :] = (o1 * (A_HEAD_DIM ** -0.5 * LOG2_E)).astype(BF16)
        aqt_ref[r + 32:r + 64, :] = (o2 * (A_HEAD_DIM ** -0.5 * LOG2_E)).astype(BF16)
    for h in range(IDX_HEADS):
        o1, o2 = rope_head(_T_IQ + h * IDX_DIM, IDX_DIM, cos64, sin64)
        r = h * IDX_DIM
        iqt_ref[r:r + 32, :] = (o1 * (IDX_DIM ** -0.5)).astype(BF16)
        iqt_ref[r + 32:r + 64, :] = (o2 * (IDX_DIM ** -0.5)).astype(BF16)

    for h in range(R_HEADS):
        o1, o2 = rope_head(_T_RQ + h * R_QK_DIM, R_QK_DIM, cos128, sin128)
        q = jnp.concatenate([o1, o2], axis=0)
        rq_ref[:, h * R_QK_DIM:(h + 1) * R_QK_DIM] = q.T.astype(BF16)
        k1, k2 = rope_head(_T_RK + h * R_QK_DIM, R_QK_DIM, cos128, sin128)
        r = h * R_QK_DIM
        rkt_ref[r:r + 64, :] = (k1 * (R_QK_DIM ** -0.5)).astype(BF16)
        rkt_ref[r + 64:r + 128, :] = (k2 * (R_QK_DIM ** -0.5)).astype(BF16)

    a1, a2 = rope_head(_T_AK, A_HEAD_DIM, cos64, sin64)
    i1, i2 = rope_head(_T_IK, IDX_DIM, cos64, sin64)
    kk = jnp.concatenate([a1, a2, i1, i2], axis=0)
    kk_ref[...] = kk.T.astype(BF16)

    avt_ref[0:A_HEAD_DIM, :] = zt[_T_AV:_T_AV + A_HEAD_DIM].astype(BF16)
    avt_ref[A_HEAD_DIM:AV_ROWS, :] = jnp.ones((AV_ROWS - A_HEAD_DIM, tm), BF16)
    iwt_ref[...] = zt[_T_IW:_T_IW + IDX_HEADS] * (IDX_HEADS ** -0.5)


def _inproj(h, g, pos3, wt, wn, tm):
    t = h.shape[0]
    nt = t // tm
    row = lambda i: (i, 0)
    col = lambda i: (0, i)
    outs = [
        (jax.ShapeDtypeStruct((A_WIDTH, t), BF16), pl.BlockSpec((A_WIDTH, tm), col)),
        (jax.ShapeDtypeStruct((A_WIDTH, t), BF16), pl.BlockSpec((A_WIDTH, tm), col)),
        (jax.ShapeDtypeStruct((t, 512), BF16), pl.BlockSpec((tm, 512), row)),
        (jax.ShapeDtypeStruct((512, t), BF16), pl.BlockSpec((512, tm), col)),
        (jax.ShapeDtypeStruct((t, 128), BF16), pl.BlockSpec((tm, 128), row)),
        (jax.ShapeDtypeStruct((AV_ROWS, t), BF16), pl.BlockSpec((AV_ROWS, tm), col)),
        (jax.ShapeDtypeStruct((IDX_HEADS, t), F32), pl.BlockSpec((IDX_HEADS, tm), col)),
        (jax.ShapeDtypeStruct((t, R_WIDTH), BF16), pl.BlockSpec((tm, R_WIDTH), row)),
        (jax.ShapeDtypeStruct((t, R_WIDTH), BF16), pl.BlockSpec((tm, R_WIDTH), row)),
        (jax.ShapeDtypeStruct((t, D_MODEL), BF16), pl.BlockSpec((tm, D_MODEL), row)),
        (jax.ShapeDtypeStruct((t, D_MODEL), BF16), pl.BlockSpec((tm, D_MODEL), row)),
    ]
    return pl.pallas_call(
        _inproj_kernel,
        grid=(nt,),
        in_specs=[
            pl.BlockSpec((tm, D_MODEL), row),
            _const_spec((1, D_MODEL)),
            pl.BlockSpec((1, 1, tm), lambda i: (i, 0, 0)),
            _const_spec((_T_ROWS, D_MODEL)),
            _const_spec((D_MODEL, _N_TOKEN_MAJOR)),
        ],
        out_specs=[o[1] for o in outs],
        out_shape=[o[0] for o in outs],
        compiler_params=pltpu.CompilerParams(
            dimension_semantics=("parallel",), vmem_limit_bytes=VMEM_LIMIT),
        name="inproj",
    )(h, g, pos3, wt, wn)


_BT_MASKS = {16: 0x0000FFFF, 8: 0x00FF00FF, 4: 0x0F0F0F0F, 2: 0x33333333, 1: 0x55555555}


def _bit_transpose32(rows):
    a = list(rows)
    j = 16
    while j:
        mask = _BT_MASKS[j]
        k = 0
        while k < WORD:
            t = (a[k] ^ (a[k + j] >> j)) & mask
            a[k] = a[k] ^ t
            a[k + j] = a[k + j] ^ (t << j)
            k = (k + j + 1) & ~j
        j >>= 1
    return a


def _dsa_kernel(kk_ref, avt_ref, aqt_ref, iqt_ref, iwt_ref, o_ref,
                lg_ref, planes_ref, posp_ref, sc_ref, thr_ref, pmax_ref, qa_ref, qi_ref,
                acc_ref, ml_ref, *, n_sel, seq):
    nq = DSA_QB
    words = seq // WORD
    wpt = KEY_TILE // WORD
    pos_bits = (seq - 1).bit_length()

    @pl.when(pl.program_id(0) == 0)
    def _():
        qa_ref[64:128, :] = jnp.zeros((64, A_HEADS * nq), BF16)
        qi_ref[0:64, :] = jnp.zeros((64, A_HEADS * nq), BF16)
        planes_ref[...] = jnp.zeros(planes_ref.shape, I32)
        for t in range(seq // KEY_TILE):
            sub = lax.broadcasted_iota(I32, (wpt, nq), 0)
            p = _bit_transpose32([(seq - 1 - t * KEY_TILE - m * wpt) - sub for m in range(WORD)])
            for i in range(pos_bits):
                posp_ref[i, t * wpt:(t + 1) * wpt, :] = p[WORD - pos_bits + i]

    def query_block(j, carry):
        _dsa_scores(j, kk_ref, aqt_ref, iqt_ref, iwt_ref, lg_ref, planes_ref, sc_ref, qa_ref, qi_ref, ml_ref)

        wrow = lax.broadcasted_iota(I32, (words, nq), 0)
        zero = jnp.zeros((1, nq), I32)
        alive = jnp.where(wrow < (j + 1) * wpt, jnp.int32(-1), jnp.int32(0))
        alive, above, key = lax.fori_loop(
            0, WORD, lambda i, s: _radix_step(planes_ref[i], s, n_sel), (alive, zero, zero))
        _, _, rev = lax.fori_loop(
            0, pos_bits, lambda i, s: _radix_step(posp_ref[i], s, n_sel), (alive, above, zero))

        _confirm_threshold(j + 1, _key_to_score(key), (seq - 1) - rev, sc_ref, thr_ref, pmax_ref, n_sel, seq)
        _dsa_attend(j, avt_ref, o_ref, lg_ref, sc_ref, thr_ref, pmax_ref, acc_ref, ml_ref)
        return carry

    lax.fori_loop(0, seq // nq, query_block, 0)


def _radix_step(plane, state, n_sel):
    alive, above, bits = state
    ones = alive & plane
    c = jnp.sum(lax.population_count(ones), axis=0, keepdims=True)
    keep_ones = (above + c) >= n_sel
    return (jnp.where(keep_ones, ones, alive ^ ones),
            jnp.where(keep_ones, above, above + c),
            (bits << 1) | jnp.where(keep_ones, 1, 0))


def _key_to_score(key):
    bits = jnp.where(key < 0, key ^ jnp.int32(INT_MIN), ~key)
    return pltpu.bitcast(bits, F32)


def _confirm_threshold(n_tiles, thr0, pmax0, sc_ref, thr_ref, pmax_ref, n_sel, seq):
    nq = thr0.shape[1]
    tile_rows, _, _ = _tile_helpers(n_tiles)
    k = float(n_sel)

    def fold(x, op):
        return op(x.reshape(KEY_TILE // 8, 8, nq), axis=0)

    def survey(thr):
        def tile(t, acc):
            n_gt, n_ge, n_tie = acc
            s = sc_ref[tile_rows(t), :]
            pos = t * KEY_TILE + lax.broadcasted_iota(I32, (KEY_TILE, nq), 0)
            return (n_gt + fold(jnp.where(s > thr, 1.0, 0.0), jnp.sum),
                    n_ge + fold(jnp.where(s >= thr, 1.0, 0.0), jnp.sum),
                    n_tie + fold(jnp.where((s == thr) & (pos <= pmax0), 1.0, 0.0), jnp.sum))
        zero8 = jnp.zeros((8, nq), F32)
        n_gt, n_ge, n_tie = (jnp.sum(x, axis=0, keepdims=True)
                             for x in lax.fori_loop(0, n_tiles, tile, (zero8, zero8, zero8)))
        off = jnp.max(jnp.where((n_gt >= k) | (n_ge < k), 1.0, 0.0)) > 0.0
        return n_gt, n_ge, n_tie, off

    def neighbours(thr):
        def tile(t, acc):
            above, below = acc
            s = sc_ref[tile_rows(t), :]
            return (jnp.minimum(above, fold(jnp.where(s > thr, s, F32_MAX), jnp.min)),
                    jnp.maximum(below, fold(jnp.where(s >= thr, -F32_MAX, s), jnp.max)))
        above, below = lax.fori_loop(
            0, n_tiles, tile, (jnp.full((8, nq), F32_MAX, F32), jnp.full((8, nq), -F32_MAX, F32)))
        return jnp.min(above, axis=0, keepdims=True), jnp.max(below, axis=0, keepdims=True)

    def step(state):
        thr, n_gt, n_ge = state[0], state[1], state[2]
        above, below = neighbours(thr)
        thr = jnp.where(n_gt >= k, above, jnp.where(n_ge < k, below, thr))
        return (thr,) + survey(thr)

    thr, n_gt, n_ge, n_tie, _ = lax.while_loop(lambda s: s[4], step, (thr0,) + survey(thr0))

    want = k - n_gt
    take_all = (n_ge - n_gt) == want
    live = thr > 0.5 * NEG_BIG
    redo = live & jnp.logical_not(take_all) & (n_tie != want)
    thr_ref[0:1, :] = thr
    thr_ref[1:2, :] = jnp.maximum(thr, 0.5 * NEG_BIG)
    pmax_ref[0:1, :] = jnp.where(live, jnp.where(take_all, jnp.int32(seq), pmax0), jnp.int32(-1))

    @pl.when(jnp.max(jnp.where(redo, 1.0, 0.0)) > 0.0)
    def _():
        def count_before(bound):
            def tile(t, acc):
                s = sc_ref[tile_rows(t), :]
                pos = t * KEY_TILE + lax.broadcasted_iota(I32, (KEY_TILE, nq), 0)
                return acc + fold(jnp.where((s == thr) & (pos < bound), 1.0, 0.0), jnp.sum)
            return jnp.sum(lax.fori_loop(0, n_tiles, tile, jnp.zeros((8, nq), F32)), axis=0, keepdims=True)

        def bisect(i, p):
            cand = p + jnp.left_shift(jnp.int32(1), (seq - 1).bit_length() - 1 - i)
            return jnp.where(count_before(cand) < want, cand, p)

        p = lax.fori_loop(0, (seq - 1).bit_length(), bisect, jnp.zeros((1, nq), I32))
        pmax_ref[0:1, :] = jnp.where(redo, p, pmax_ref[0:1, :])


def _tile_helpers(n_tiles):
    wpt = KEY_TILE // WORD

    def tile_rows(t):
        return pl.ds(pl.multiple_of(t * KEY_TILE, KEY_TILE), KEY_TILE)

    def word_rows(t):
        return pl.ds(pl.multiple_of(t * wpt, wpt), wpt)

    def for_each_tile(body):
        def group(i, carry):
            for u in range(TILE_UNROLL):
                body(TILE_UNROLL * i + u)
            return carry

        lax.fori_loop(0, n_tiles // TILE_UNROLL, group, 0)
        done = n_tiles - n_tiles % TILE_UNROLL
        width = TILE_UNROLL // 2
        while width:
            @pl.when((n_tiles // width) % 2 == 1)
            def _(done=done, width=width):
                for u in range(width):
                    body(done + u)
            done = done + (n_tiles // width) % 2 * width
            width //= 2

    return tile_rows, word_rows, for_each_tile


def _dsa_scores(j, kk_ref, aqt_ref, iqt_ref, iwt_ref, lg_ref, planes_ref, sc_ref, qa_ref, qi_ref, ml_ref):
    nq = DSA_QB
    tile_rows, word_rows, for_each_tile = _tile_helpers(j + 1)
    queries = pl.ds(pl.multiple_of(j * nq, nq), nq)

    for h in range(A_HEADS):
        cs = slice(h * nq, (h + 1) * nq)
        qa_ref[0:64, cs] = aqt_ref[h * 64:(h + 1) * 64, queries]
        qi_ref[64:128, cs] = iqt_ref[h * 64:(h + 1) * 64, queries]

    lane = lax.broadcasted_iota(I32, (1, nq), 1)
    vis_end = (j * nq + (lane // CHUNK + 1) * CHUNK)
    w = iwt_ref[:, queries]
    ml_ref[1:2, :] = jnp.full((1, A_HEADS * nq), 0.5 * NEG_BIG, F32)

    def score_tile(t):
        keys = []
        for half in range(2):
            r0 = pl.multiple_of(t * KEY_TILE + half * KEY_HALF, KEY_HALF)
            kt = kk_ref[pl.ds(r0, KEY_HALF), :]
            sc = None
            for h in range(IDX_HEADS):
                rel = jnp.maximum(_dot(kt, qi_ref[:, h * nq:(h + 1) * nq]), 0.0)
                part = rel * w[h:h + 1, :]
                sc = part if sc is None else sc + part
            sc = jnp.where(sc == 0.0, 0.0, sc)
            spos = r0 + lax.broadcasted_iota(I32, (KEY_HALF, nq), 0)
            sc = jnp.where(spos < vis_end, sc, NEG_BIG)
            sc_ref[pl.ds(r0, KEY_HALF), :] = sc
            bits = pltpu.bitcast(sc, I32)
            keys.append(bits ^ ((bits >> 31) | jnp.int32(INT_MIN)))

        kt_all = kk_ref[tile_rows(t), :]
        for h in range(A_HEADS):
            cs = slice(h * nq, (h + 1) * nq)
            lg = _dot(kt_all, qa_ref[:, cs])
            lg_ref[tile_rows(t), cs] = lg
            ml_ref[1:2, cs] = jnp.maximum(ml_ref[1:2, cs], jnp.max(lg, axis=0, keepdims=True))

        def member(parts, m):
            r = (m % 16) * 8
            return parts[m // 16][r:r + 8, :]

        planes = _bit_transpose32([member(keys, m) for m in range(WORD)])
        for i in range(WORD):
            planes_ref[i, word_rows(t), :] = planes[i]

    for_each_tile(score_tile)


def _dsa_attend(j, avt_ref, o_ref, lg_ref, sc_ref, thr_ref, pmax_ref, acc_ref, ml_ref):
    nq = DSA_QB
    n_tiles = j + 1
    tile_rows, word_rows, for_each_tile = _tile_helpers(n_tiles)
    queries = pl.ds(pl.multiple_of(j * nq, nq), nq)

    def masked_logits(t, cs, bias):
        return lg_ref[tile_rows(t), cs] + bias

    thr, thr_valid, pmax = thr_ref[0:1, :], thr_ref[1:2, :], pmax_ref[0:1, :]

    def mask_bias(t):
        s = sc_ref[tile_rows(t), :]
        pos = t * KEY_TILE + lax.broadcasted_iota(I32, (KEY_TILE, nq), 0)
        selected = (s > thr_valid) | ((s == thr) & (pos <= pmax))
        return jnp.where(selected, 0.0, NEG_BIG)

    def attend(shift_row):
        acc_ref[...] = jnp.zeros(acc_ref.shape, F32)

        def attn_tile(t):
            vt = avt_ref[:, tile_rows(t)]
            bias = mask_bias(t)
            for h in range(A_HEADS):
                cs = slice(h * nq, (h + 1) * nq)
                p = jnp.exp2(masked_logits(t, cs, bias) - ml_ref[shift_row:shift_row + 1, cs])
                acc_ref[:, cs] = acc_ref[:, cs] + _dot(vt, p.astype(BF16))

        for_each_tile(attn_tile)

    attend(1)
    underflow = jnp.min(acc_ref[A_HEAD_DIM:A_HEAD_DIM + 1, :]) < DENOM_MIN

    @pl.when(underflow)
    def _():
        ml_ref[0:1, :] = jnp.full((1, A_HEADS * nq), 0.5 * NEG_BIG, F32)

        def max_tile(t, carry):
            bias = mask_bias(t)
            for h in range(A_HEADS):
                cs = slice(h * nq, (h + 1) * nq)
                ml_ref[0:1, cs] = jnp.maximum(
                    ml_ref[0:1, cs], jnp.max(masked_logits(t, cs, bias), axis=0, keepdims=True))
            return carry

        lax.fori_loop(0, n_tiles, max_tile, 0)
        attend(0)

    out_t = acc_ref[0:A_HEAD_DIM, :] / acc_ref[A_HEAD_DIM:A_HEAD_DIM + 1, :]
    for hp in range(A_HEADS // 2):
        blk = jnp.concatenate(
            [out_t[:, (2 * hp) * nq:(2 * hp + 1) * nq], out_t[:, (2 * hp + 1) * nq:(2 * hp + 2) * nq]], axis=0)
        o_ref[queries, hp * 128:(hp + 1) * 128] = blk.T.astype(o_ref.dtype)


def _dsa(kk, avt, aqt, iqt, iwt, batch, seq, n_sel):
    t = batch * seq
    feat = lambda b: (0, b)
    return pl.pallas_call(
        functools.partial(_dsa_kernel, n_sel=n_sel, seq=seq),
        grid=(batch,),
        in_specs=[
            pl.BlockSpec((seq, 128), lambda b: (b, 0)),
            pl.BlockSpec((AV_ROWS, seq), feat),
            pl.BlockSpec((A_WIDTH, seq), feat),
            pl.BlockSpec((A_WIDTH, seq), feat),
            pl.BlockSpec((IDX_HEADS, seq), feat),
        ],
        out_specs=pl.BlockSpec((seq, A_WIDTH), lambda b: (b, 0)),
        out_shape=jax.ShapeDtypeStruct((t, A_WIDTH), BF16),
        scratch_shapes=[
            pltpu.VMEM((seq, A_HEADS * DSA_QB), F32),
            pltpu.VMEM((WORD, seq // WORD, DSA_QB), I32),
            pltpu.VMEM(((seq - 1).bit_length(), seq // WORD, DSA_QB), I32),
            pltpu.VMEM((seq, DSA_QB), F32),
            pltpu.VMEM((8, DSA_QB), F32),
            pltpu.VMEM((8, DSA_QB), I32),
            pltpu.VMEM((128, A_HEADS * DSA_QB), BF16),
            pltpu.VMEM((128, A_HEADS * DSA_QB), BF16),
            pltpu.VMEM((AV_ROWS, A_HEADS * DSA_QB), F32),
            pltpu.VMEM((8, A_HEADS * DSA_QB), F32),
        ],
        compiler_params=pltpu.CompilerParams(
            dimension_semantics=("arbitrary",), vmem_limit_bytes=VMEM_LIMIT),
        name="dsa",
    )(kk, avt, aqt, iqt, iwt)


def _log_gamma(h):
    return math.log1p(-(2.0 ** (-5.0 - h)))


def _retention_kernel(rq_ref, rkt_ref, rv_ref, rg_ref, gn_ref, o_ref,
                      state_ref, dmat_ref, dq_ref):
    nb = RET_BLOCK

    @pl.when(pl.program_id(0) == 0)
    def _():
        i = lax.broadcasted_iota(I32, (nb, nb), 0)
        jj = lax.broadcasted_iota(I32, (nb, nb), 1)
        dist = jnp.abs(i - jj).astype(F32)
        seen = (jj // CHUNK) <= (i // CHUNK)
        for h in range(R_HEADS):
            lg = _log_gamma(h)
            dmat_ref[h] = jnp.where(seen, jnp.exp(lg * dist), 0.0)
            dq_ref[h] = jnp.exp(lg * (i.astype(F32) + 1.0))

    state_ref[...] = jnp.zeros(state_ref.shape, F32)
    jrow = lax.broadcasted_iota(I32, (1, nb), 1).astype(F32)

    def block(n, carry):
        toks = pl.ds(pl.multiple_of(n * nb, nb), nb)
        for h in range(R_HEADS):
            lg = _log_gamma(h)
            q = rq_ref[toks, h * R_QK_DIM:(h + 1) * R_QK_DIM]
            kt = rkt_ref[h * R_QK_DIM:(h + 1) * R_QK_DIM, toks]
            v = rv_ref[toks, h * R_V_DIM:(h + 1) * R_V_DIM]
            state = state_ref[h]
            s = _dot(q, kt) * dmat_ref[h]
            y = _dot(s.astype(BF16), v) + _dot(q, state.astype(BF16)) * dq_ref[h]
            dk = jnp.exp(lg * (nb - 1.0 - jrow))
            kd = (kt.astype(F32) * dk).astype(BF16)
            state_ref[h] = state * math.exp(lg * nb) + _dot(kd, v)
            mean = jnp.mean(y, axis=-1, keepdims=True)
            yc = y - mean
            var = jnp.mean(yc * yc, axis=-1, keepdims=True)
            yn = yc * lax.rsqrt(var + GN_EPS) * gn_ref[:, h * R_V_DIM:(h + 1) * R_V_DIM]
            gate = rg_ref[toks, h * R_V_DIM:(h + 1) * R_V_DIM].astype(F32)
            o_ref[toks, h * R_V_DIM:(h + 1) * R_V_DIM] = (yn * (gate * jax.nn.sigmoid(gate))).astype(o_ref.dtype)
        return carry

    lax.fori_loop(0, rq_ref.shape[0] // nb, block, 0)


def _retention(rq, rkt, rv, rg, gn, batch, seq):
    t = batch * seq
    tok = lambda b: (b, 0)
    return pl.pallas_call(
        _retention_kernel,
        grid=(batch,),
        in_specs=[
            pl.BlockSpec((seq, 512), tok),
            pl.BlockSpec((512, seq), lambda b: (0, b)),
            pl.BlockSpec((seq, R_WIDTH), tok),
            pl.BlockSpec((seq, R_WIDTH), tok),
            pl.BlockSpec((1, R_WIDTH), lambda b: (0, 0)),
        ],
        out_specs=pl.BlockSpec((seq, R_WIDTH), tok),
        out_shape=jax.ShapeDtypeStruct((t, R_WIDTH), BF16),
        scratch_shapes=[
            pltpu.VMEM((R_HEADS, R_QK_DIM, R_V_DIM), F32),
            pltpu.VMEM((R_HEADS, RET_BLOCK, RET_BLOCK), F32),
            pltpu.VMEM((R_HEADS, RET_BLOCK, RET_BLOCK), F32),
        ],
        compiler_params=pltpu.CompilerParams(
            dimension_semantics=("arbitrary",), vmem_limit_bytes=VMEM_LIMIT),
        name="retention",
    )(rq, rkt, rv, rg, gn)


def _tail_kernel(h_ref, ya_ref, yr_ref, ga_ref, gb_ref, p_ref,
                 wa_ref, wb_ref, wo_ref, g2_ref, wg_ref, wu_ref, wd_ref,
                 gp_ref, wpg_ref, wpp_ref, gf_ref, o_ref):
    a = _dot(ya_ref[...], wa_ref[...])
    b = _dot(yr_ref[...], wb_ref[...])
    merged = (jax.nn.sigmoid(ga_ref[...].astype(F32)) * a
              + jax.nn.sigmoid(gb_ref[...].astype(F32)) * b)
    h = h_ref[...] + _dot(merged.astype(BF16), wo_ref[...])
    h = h + 0.5 * _swiglu(_rms(h, g2_ref[...]).astype(BF16), wg_ref, wu_ref, wd_ref)
    gate = jax.nn.sigmoid(_dot(_rms(h, gp_ref[...]).astype(BF16), wpg_ref[...]))
    emb = _dot(p_ref[...].astype(BF16), wpp_ref[...])
    o_ref[...] = _rms(h + gate * emb, gf_ref[...])


def _tail(h, ya, yr, ga, gb, p, wa, wb, wo, g2, wg, wu, wd, gp, wpg, wpp, gf, tm):
    t = h.shape[0]
    row = lambda i: (i, 0)
    return pl.pallas_call(
        _tail_kernel,
        grid=(t // tm,),
        in_specs=[
            pl.BlockSpec((tm, D_MODEL), row),
            pl.BlockSpec((tm, A_WIDTH), row),
            pl.BlockSpec((tm, R_WIDTH), row),
            pl.BlockSpec((tm, D_MODEL), row),
            pl.BlockSpec((tm, D_MODEL), row),
            pl.BlockSpec((tm, PLE_DIM), row),
            _const_spec((A_WIDTH, D_MODEL)),
            _const_spec((R_WIDTH, D_MODEL)),
            _const_spec((D_MODEL, D_MODEL)),
            _const_spec((1, D_MODEL)),
            _const_spec((D_MODEL, D_FF)),
            _const_spec((D_MODEL, D_FF)),
            _const_spec((D_FF, D_MODEL)),
            _const_spec((1, D_MODEL)),
            _const_spec((D_MODEL, D_MODEL)),
            _const_spec((PLE_DIM, D_MODEL)),
            _const_spec((1, D_MODEL)),
        ],
        out_specs=pl.BlockSpec((tm, D_MODEL), row),
        out_shape=jax.ShapeDtypeStruct((t, D_MODEL), F32),
        compiler_params=pltpu.CompilerParams(
            dimension_semantics=("parallel",), vmem_limit_bytes=VMEM_LIMIT),
        name="tail",
    )(h, ya, yr, ga, gb, p, wa, wb, wo, g2, wg, wu, wd, gp, wpg, wpp, gf)


def _token_tile(t):
    return 512 if t % 512 == 0 else 256


def kernel(x, p, positions, ffn1_norm, ffn1_w_gate, ffn1_w_up, ffn1_w_down, mix_norm, w_in, ret_gn, w_branch_a, w_branch_b, w_out, ffn2_norm, ffn2_w_gate, ffn2_w_up, ffn2_w_down, ple_norm, w_ple_gate, w_ple_proj, final_norm):
    batch, seq, _ = x.shape
    depth = p.shape[0]
    t = batch * seq
    tm = _token_tile(t)
    n_sel = min(TOPK_MAX, seq // 4)
    assert depth == 1, "the final norm is fused into the per-layer embedding step"
    assert seq % RET_BLOCK == 0 and seq % KEY_TILE == 0 and seq % DSA_QB == 0
    assert n_sel <= KEY_TILE and DSA_QB == KEY_TILE and KEY_TILE == 8 * WORD

    h = x.reshape(t, D_MODEL)
    pos3 = positions.reshape(t // tm, 1, tm)
    vec = lambda g: g.reshape(1, -1).astype(F32)

    for i in range(depth):
        w = w_in[i].astype(BF16)
        wt = jnp.concatenate([
            w[:, _OFF_AQ:_OFF_AQ + A_WIDTH], w[:, _OFF_IQ:_OFF_IQ + IDX_HEADS * IDX_DIM],
            w[:, _OFF_RQ:_OFF_RQ + 512], w[:, _OFF_RK:_OFF_RK + 512],
            w[:, _OFF_AK:_OFF_AK + 64], w[:, _OFF_IK:_OFF_IK + 64],
            w[:, _OFF_AV:_OFF_AV + 64], w[:, _OFF_IW:_OFF_IW + IDX_HEADS]], axis=1).T
        wn = w[:, _OFF_RV:]

        h = _ffn(h, vec(ffn1_norm[i]), ffn1_w_gate[i].astype(BF16), ffn1_w_up[i].astype(BF16),
                 ffn1_w_down[i].astype(BF16), tm)
        aqt, iqt, rq, rkt, kk, avt, iwt, rv, rg, ga, gb = _inproj(h, vec(mix_norm[i]), pos3, wt, wn, tm)
        ya = _dsa(kk, avt, aqt, iqt, iwt, batch, seq, n_sel)
        yr = _retention(rq, rkt, rv, rg, vec(ret_gn[i]), batch, seq)
        h = _tail(h, ya, yr, ga, gb, p[i].reshape(t, PLE_DIM),
                  w_branch_a[i].astype(BF16), w_branch_b[i].astype(BF16), w_out[i].astype(BF16),
                  vec(ffn2_norm[i]), ffn2_w_gate[i].astype(BF16), ffn2_w_up[i].astype(BF16),
                  ffn2_w_down[i].astype(BF16),
                  vec(ple_norm[i]), w_ple_gate[i].astype(BF16), w_ple_proj[i].astype(BF16),
                  vec(final_norm), tm)
    return h.reshape(batch, seq, D_MODEL)
```

```python
import functools
import math

import jax
import jax.numpy as jnp
from jax import lax
from jax.experimental import pallas as pl
from jax.experimental.pallas import tpu as pltpu

F32 = jnp.float32
BF16 = jnp.bfloat16
I32 = jnp.int32

D_MODEL = 1024
CHUNK = 64
ROPE_THETA = 10000.0
EPS = 1e-6
GN_EPS = 1e-5
NEG_BIG = -1e30
A_HEADS = 8
A_HEAD_DIM = 64
IDX_HEADS = 8
IDX_DIM = 64
TOPK_MAX = 256
A_WIDTH = A_HEADS * A_HEAD_DIM
R_HEADS = 4
R_QK_DIM = 128
R_V_DIM = 256
R_WIDTH = R_HEADS * R_V_DIM
D_FF = 2816
PLE_DIM = 256

_OFF_AQ = 0
_OFF_AK = _OFF_AQ + A_WIDTH
_OFF_AV = _OFF_AK + A_HEAD_DIM
_OFF_IQ = _OFF_AV + A_HEAD_DIM
_OFF_IK = _OFF_IQ + IDX_HEADS * IDX_DIM
_OFF_IW = _OFF_IK + IDX_DIM
_OFF_RQ = _OFF_IW + IDX_HEADS
_OFF_RK = _OFF_RQ + R_HEADS * R_QK_DIM
_OFF_RV = _OFF_RK + R_HEADS * R_QK_DIM
_N_TOKEN_MAJOR = 2 * R_WIDTH + 2 * D_MODEL

_T_AQ = 0
_T_IQ = _T_AQ + A_WIDTH
_T_RQ = _T_IQ + IDX_HEADS * IDX_DIM
_T_RK = _T_RQ + R_HEADS * R_QK_DIM
_T_AK = _T_RK + R_HEADS * R_QK_DIM
_T_IK = _T_AK + A_HEAD_DIM
_T_AV = _T_IK + IDX_DIM
_T_IW = _T_AV + A_HEAD_DIM
_T_ROWS = _T_IW + IDX_HEADS

VMEM_LIMIT = 56 * 1024 * 1024
FF_CHUNK = 256
DSA_QB = 256
KEY_TILE = 256
KEY_HALF = KEY_TILE // 2
TILE_UNROLL = 4
RET_BLOCK = 256

AV_ROWS = A_HEAD_DIM + 16
INT_MIN = -(2 ** 31)
F32_MAX = 3.0e38
WORD = 32
LOG2_E = math.log2(math.e)
DENOM_MIN = 2.0 ** -100


def _const_spec(shape):
    nd = len(shape)
    return pl.BlockSpec(shape, lambda *_: (0,) * nd, pipeline_mode=pl.Buffered(1))


def _rms(x, g):
    ms = jnp.mean(x * x, axis=-1, keepdims=True)
    return x * lax.rsqrt(ms + EPS) * g


def _dot(a, b):
    return jnp.dot(a, b, preferred_element_type=F32)


def _swiglu(u, wg_ref, wu_ref, wd_ref):
    acc = jnp.zeros((u.shape[0], D_MODEL), F32)
    for c in range(D_FF // FF_CHUNK):
        sl = slice(c * FF_CHUNK, (c + 1) * FF_CHUNK)
        gate = _dot(u, wg_ref[:, sl])
        up = _dot(u, wu_ref[:, sl])
        act = (gate * jax.nn.sigmoid(gate) * up).astype(BF16)
        acc = acc + _dot(act, wd_ref[sl, :])
    return acc


def _ffn_kernel(h_ref, g_ref, wg_ref, wu_ref, wd_ref, o_ref):
    h = h_ref[...]
    o_ref[...] = h + 0.5 * _swiglu(_rms(h, g_ref[...]).astype(BF16), wg_ref, wu_ref, wd_ref)


def _ffn(h, g, wg, wu, wd, tm):
    t = h.shape[0]
    return pl.pallas_call(
        _ffn_kernel,
        grid=(t // tm,),
        in_specs=[
            pl.BlockSpec((tm, D_MODEL), lambda i: (i, 0)),
            _const_spec((1, D_MODEL)),
            _const_spec((D_MODEL, D_FF)),
            _const_spec((D_MODEL, D_FF)),
            _const_spec((D_FF, D_MODEL)),
        ],
        out_specs=pl.BlockSpec((tm, D_MODEL), lambda i: (i, 0)),
        out_shape=jax.ShapeDtypeStruct((t, D_MODEL), F32),
        compiler_params=pltpu.CompilerParams(
            dimension_semantics=("parallel",), vmem_limit_bytes=VMEM_LIMIT),
        name="ffn",
    )(h, g, wg, wu, wd)


def _rope_rows(t1, t2, cos, sin):
    return t1 * cos - t2 * sin, t2 * cos + t1 * sin


def _inproj_kernel(h_ref, g_ref, pos_ref, wt_ref, wn_ref,
                   aqt_ref, iqt_ref, rq_ref, rkt_ref, kk_ref, avt_ref, iwt_ref,
                   rv_ref, rg_ref, ga_ref, gb_ref):
    tm = h_ref.shape[0]
    u = _rms(h_ref[...], g_ref[...]).astype(BF16)

    zn = _dot(u, wn_ref[...])
    rv_ref[...] = zn[:, 0:R_WIDTH].astype(BF16)
    rg_ref[...] = zn[:, R_WIDTH:2 * R_WIDTH].astype(BF16)
    ga_ref[...] = zn[:, 2 * R_WIDTH:2 * R_WIDTH + D_MODEL].astype(BF16)
    gb_ref[...] = zn[:, 2 * R_WIDTH + D_MODEL:].astype(BF16)

    zt = lax.dot_general(wt_ref[...], u, (((1,), (1,)), ((), ())),
                         preferred_element_type=F32)

    pos = pos_ref[0].astype(F32)

    def angles(half):
        i = lax.broadcasted_iota(I32, (half, tm), 0).astype(F32)
        inv = jnp.exp(i * (-math.log(ROPE_THETA) / half))
        ang = pos * inv
        return jnp.cos(ang), jnp.sin(ang)

    cos64, sin64 = angles(A_HEAD_DIM // 2)
    cos128, sin128 = angles(R_QK_DIM // 2)

    def rope_head(row0, dim, cos, sin):
        half = dim // 2
        return _rope_rows(zt[row0:row0 + half], zt[row0 + half:row0 + dim], cos, sin)

    for h in range(A_HEADS):
        o1, o2 = rope_head(_T_AQ + h * A_HEAD_DIM, A_HEAD_DIM, cos64, sin64)
        r = h * A_HEAD_DIM
        aqt_ref[r:r + 32, :] = (o1 * (A_HEAD_DIM ** -0.5 * LOG2_E)).astype(BF16)
        aqt_ref[r + 32:r + 64, :] = (o2 * (A_HEAD_DIM ** -0.5 * LOG2_E)).astype(BF16)
    for h in range(IDX_HEADS):
        o1, o2 = rope_head(_T_IQ + h * IDX_DIM, IDX_DIM, cos64, sin64)
        r = h * IDX_DIM
        iqt_ref[r:r + 32, :] = (o1 * (IDX_DIM ** -0.5)).astype(BF16)
        iqt_ref[r + 32:r + 64, :] = (o2 * (IDX_DIM ** -0.5)).astype(BF16)

    for h in range(R_HEADS):
        o1, o2 = rope_head(_T_RQ + h * R_QK_DIM, R_QK_DIM, cos128, sin128)
        q = jnp.concatenate([o1, o2], axis=0)
        rq_ref[:, h * R_QK_DIM:(h + 1) * R_QK_DIM] = q.T.astype(BF16)
        k1, k2 = rope_head(_T_RK + h * R_QK_DIM, R_QK_DIM, cos128, sin128)
        r = h * R_QK_DIM
        rkt_ref[r:r + 64, :] = (k1 * (R_QK_DIM ** -0.5)).astype(BF16)
        rkt_ref[r + 64:r + 128, :] = (k2 * (R_QK_DIM ** -0.5)).astype(BF16)

    a1, a2 = rope_head(_T_AK, A_HEAD_DIM, cos64, sin64)
    i1, i2 = rope_head(_T_IK, IDX_DIM, cos64, sin64)
    kk = jnp.concatenate([a1, a2, i1, i2], axis=0)
    kk_ref[...] = kk.T.astype(BF16)

    avt_ref[0:A_HEAD_DIM, :] = zt[_T_AV:_T_AV + A_HEAD_DIM].astype(BF16)
    avt_ref[A_HEAD_DIM:AV_ROWS, :] = jnp.ones((AV_ROWS - A_HEAD_DIM, tm), BF16)
    iwt_ref[...] = zt[_T_IW:_T_IW + IDX_HEADS] * (IDX_HEADS ** -0.5)


def _inproj(h, g, pos3, wt, wn, tm):
    t = h.shape[0]
    nt = t // tm
    row = lambda i: (i, 0)
    col = lambda i: (0, i)
    outs = [
        (jax.ShapeDtypeStruct((A_WIDTH, t), BF16), pl.BlockSpec((A_WIDTH, tm), col)),
        (jax.ShapeDtypeStruct((A_WIDTH, t), BF16), pl.BlockSpec((A_WIDTH, tm), col)),
        (jax.ShapeDtypeStruct((t, 512), BF16), pl.BlockSpec((tm, 512), row)),
        (jax.ShapeDtypeStruct((512, t), BF16), pl.BlockSpec((512, tm), col)),
        (jax.ShapeDtypeStruct((t, 128), BF16), pl.BlockSpec((tm, 128), row)),
        (jax.ShapeDtypeStruct((AV_ROWS, t), BF16), pl.BlockSpec((AV_ROWS, tm), col)),
        (jax.ShapeDtypeStruct((IDX_HEADS, t), F32), pl.BlockSpec((IDX_HEADS, tm), col)),
        (jax.ShapeDtypeStruct((t, R_WIDTH), BF16), pl.BlockSpec((tm, R_WIDTH), row)),
        (jax.ShapeDtypeStruct((t, R_WIDTH), BF16), pl.BlockSpec((tm, R_WIDTH), row)),
        (jax.ShapeDtypeStruct((t, D_MODEL), BF16), pl.BlockSpec((tm, D_MODEL), row)),
        (jax.ShapeDtypeStruct((t, D_MODEL), BF16), pl.BlockSpec((tm, D_MODEL), row)),
    ]
    return pl.pallas_call(
        _inproj_kernel,
        grid=(nt,),
        in_specs=[
            pl.BlockSpec((tm, D_MODEL), row),
            _const_spec((1, D_MODEL)),
            pl.BlockSpec((1, 1, tm), lambda i: (i, 0, 0)),
            _const_spec((_T_ROWS, D_MODEL)),
            _const_spec((D_MODEL, _N_TOKEN_MAJOR)),
        ],
        out_specs=[o[1] for o in outs],
        out_shape=[o[0] for o in outs],
        compiler_params=pltpu.CompilerParams(
            dimension_semantics=("parallel",), vmem_limit_bytes=VMEM_LIMIT),
        name="inproj",
    )(h, g, pos3, wt, wn)


_BT_MASKS = {16: 0x0000FFFF, 8: 0x00FF00FF, 4: 0x0F0F0F0F, 2: 0x33333333, 1: 0x55555555}


def _bit_transpose32(rows):
    a = list(rows)
    j = 16
    while j:
        mask = _BT_MASKS[j]
        k = 0
        while k < WORD:
            t = (a[k] ^ (a[k + j] >> j)) & mask
            a[k] = a[k] ^ t
            a[k + j] = a[k + j] ^ (t << j)
            k = (k + j + 1) & ~j
        j >>= 1
    return a


def _dsa_kernel(kk_ref, avt_ref, aqt_ref, iqt_ref, iwt_ref, o_ref,
                lg_ref, planes_ref, posp_ref, sc_ref, thr_ref, pmax_ref, qa_ref, qi_ref,
                acc_ref, ml_ref, *, n_sel, seq):
    nq = DSA_QB
    words = seq // WORD
    wpt = KEY_TILE // WORD
    pos_bits = (seq - 1).bit_length()

    @pl.when(pl.program_id(0) == 0)
    def _():
        qa_ref[64:128, :] = jnp.zeros((64, A_HEADS * nq), BF16)
        qi_ref[0:64, :] = jnp.zeros((64, A_HEADS * nq), BF16)
        planes_ref[...] = jnp.zeros(planes_ref.shape, I32)
        for t in range(seq // KEY_TILE):
            sub = lax.broadcasted_iota(I32, (wpt, nq), 0)
            p = _bit_transpose32([(seq - 1 - t * KEY_TILE - m * wpt) - sub for m in range(WORD)])
            for i in range(pos_bits):
                posp_ref[i, t * wpt:(t + 1) * wpt, :] = p[WORD - pos_bits + i]

    def query_block(j, carry):
        _dsa_scores(j, kk_ref, aqt_ref, iqt_ref, iwt_ref, lg_ref, planes_ref, sc_ref, qa_ref, qi_ref, ml_ref)

        wrow = lax.broadcasted_iota(I32, (words, nq), 0)
        zero = jnp.zeros((1, nq), I32)
        alive = jnp.where(wrow < (j + 1) * wpt, jnp.int32(-1), jnp.int32(0))
        alive, above, key = lax.fori_loop(
            0, WORD, lambda i, s: _radix_step(planes_ref[i], s, n_sel), (alive, zero, zero))
        _, _, rev = lax.fori_loop(
            0, pos_bits, lambda i, s: _radix_step(posp_ref[i], s, n_sel), (alive, above, zero))

        _confirm_threshold(j + 1, _key_to_score(key), (seq - 1) - rev, sc_ref, thr_ref, pmax_ref, n_sel, seq)
        _dsa_attend(j, avt_ref, o_ref, lg_ref, sc_ref, thr_ref, pmax_ref, acc_ref, ml_ref)
        return carry

    lax.fori_loop(0, seq // nq, query_block, 0)


def _radix_step(plane, state, n_sel):
    alive, above, bits = state
    ones = alive & plane
    c = jnp.sum(lax.population_count(ones), axis=0, keepdims=True)
    keep_ones = (above + c) >= n_sel
    return (jnp.where(keep_ones, ones, alive ^ ones),
            jnp.where(keep_ones, above, above + c),
            (bits << 1) | jnp.where(keep_ones, 1, 0))


def _key_to_score(key):
    bits = jnp.where(key < 0, key ^ jnp.int32(INT_MIN), ~key)
    return pltpu.bitcast(bits, F32)


def _confirm_threshold(n_tiles, thr0, pmax0, sc_ref, thr_ref, pmax_ref, n_sel, seq):
    nq = thr0.shape[1]
    tile_rows, _, _ = _tile_helpers(n_tiles)
    k = float(n_sel)

    def fold(x, op):
        return op(x.reshape(KEY_TILE // 8, 8, nq), axis=0)

    def survey(thr):
        def tile(t, acc):
            n_gt, n_eq, n_tie = acc
            s = sc_ref[tile_rows(t), :]
            pos = t * KEY_TILE + lax.broadcasted_iota(I32, (KEY_TILE, nq), 0)
            eq = s == thr
            return (n_gt + fold(jnp.where(s > thr, 1.0, 0.0), jnp.sum),
                    n_eq + fold(jnp.where(eq, 1.0, 0.0), jnp.sum),
                    n_tie + fold(jnp.where(eq & (pos <= pmax0), 1.0, 0.0), jnp.sum))
        zero8 = jnp.zeros((8, nq), F32)
        n_gt, n_eq, n_tie = (jnp.sum(x, axis=0, keepdims=True)
                             for x in lax.fori_loop(0, n_tiles, tile, (zero8, zero8, zero8)))
        n_ge = n_gt + n_eq
        off = jnp.max(jnp.where((n_gt >= k) | (n_ge < k), 1.0, 0.0)) > 0.0
        return n_gt, n_ge, n_tie, off

    def neighbours(thr):
        def tile(t, acc):
            above, below = acc
            s = sc_ref[tile_rows(t), :]
            return (jnp.minimum(above, fold(jnp.where(s > thr, s, F32_MAX), jnp.min)),
                    jnp.maximum(below, fold(jnp.where(s >= thr, -F32_MAX, s), jnp.max)))
        above, below = lax.fori_loop(
            0, n_tiles, tile, (jnp.full((8, nq), F32_MAX, F32), jnp.full((8, nq), -F32_MAX, F32)))
        return jnp.min(above, axis=0, keepdims=True), jnp.max(below, axis=0, keepdims=True)

    def step(state):
        thr, n_gt, n_ge = state[0], state[1], state[2]
        above, below = neighbours(thr)
        thr = jnp.where(n_gt >= k, above, jnp.where(n_ge < k, below, thr))
        return (thr,) + survey(thr)

    thr, n_gt, n_ge, n_tie, _ = lax.while_loop(lambda s: s[4], step, (thr0,) + survey(thr0))

    want = k - n_gt
    take_all = (n_ge - n_gt) == want
    live = thr > 0.5 * NEG_BIG
    redo = live & jnp.logical_not(take_all) & (n_tie != want)
    thr_ref[0:1, :] = thr
    thr_ref[1:2, :] = jnp.maximum(thr, 0.5 * NEG_BIG)
    pmax_ref[0:1, :] = jnp.where(live, jnp.where(take_all, jnp.int32(seq), pmax0), jnp.int32(-1))

    @pl.when(jnp.max(jnp.where(redo, 1.0, 0.0)) > 0.0)
    def _():
        def count_before(bound):
            def tile(t, acc):
                s = sc_ref[tile_rows(t), :]
                pos = t * KEY_TILE + lax.broadcasted_iota(I32, (KEY_TILE, nq), 0)
                return acc + fold(jnp.where((s == thr) & (pos < bound), 1.0, 0.0), jnp.sum)
            return jnp.sum(lax.fori_loop(0, n_tiles, tile, jnp.zeros((8, nq), F32)), axis=0, keepdims=True)

        def bisect(i, p):
            cand = p + jnp.left_shift(jnp.int32(1), (seq - 1).bit_length() - 1 - i)
            return jnp.where(count_before(cand) < want, cand, p)

        p = lax.fori_loop(0, (seq - 1).bit_length(), bisect, jnp.zeros((1, nq), I32))
        pmax_ref[0:1, :] = jnp.where(redo, p, pmax_ref[0:1, :])


def _tile_helpers(n_tiles):
    wpt = KEY_TILE // WORD

    def tile_rows(t):
        return pl.ds(pl.multiple_of(t * KEY_TILE, KEY_TILE), KEY_TILE)

    def word_rows(t):
        return pl.ds(pl.multiple_of(t * wpt, wpt), wpt)

    def for_each_tile(body):
        def group(i, carry):
            for u in range(TILE_UNROLL):
                body(TILE_UNROLL * i + u)
            return carry

        lax.fori_loop(0, n_tiles // TILE_UNROLL, group, 0)
        done = n_tiles - n_tiles % TILE_UNROLL
        width = TILE_UNROLL // 2
        while width:
            @pl.when((n_tiles // width) % 2 == 1)
            def _(done=done, width=width):
                for u in range(width):
                    body(done + u)
            done = done + (n_tiles // width) % 2 * width
            width //= 2

    return tile_rows, word_rows, for_each_tile


def _dsa_scores(j, kk_ref, aqt_ref, iqt_ref, iwt_ref, lg_ref, planes_ref, sc_ref, qa_ref, qi_ref, ml_ref):
    nq = DSA_QB
    tile_rows, word_rows, for_each_tile = _tile_helpers(j + 1)
    queries = pl.ds(pl.multiple_of(j * nq, nq), nq)

    for h in range(A_HEADS):
        cs = slice(h * nq, (h + 1) * nq)
        qa_ref[0:64, cs] = aqt_ref[h * 64:(h + 1) * 64, queries]
        qi_ref[64:128, cs] = iqt_ref[h * 64:(h + 1) * 64, queries]

    lane = lax.broadcasted_iota(I32, (1, nq), 1)
    vis_end = (j * nq + (lane // CHUNK + 1) * CHUNK)
    w = iwt_ref[:, queries]
    ml_ref[1:2, :] = jnp.full((1, A_HEADS * nq), 0.5 * NEG_BIG, F32)

    def score_tile(t):
        keys = []
        for half in range(2):
            r0 = pl.multiple_of(t * KEY_TILE + half * KEY_HALF, KEY_HALF)
            kt = kk_ref[pl.ds(r0, KEY_HALF), :]
            sc = None
            for h in range(IDX_HEADS):
                rel = jnp.maximum(_dot(kt, qi_ref[:, h * nq:(h + 1) * nq]), 0.0)
                part = rel * w[h:h + 1, :]
                sc = part if sc is None else sc + part
            sc = jnp.where(sc == 0.0, 0.0, sc)
            spos = r0 + lax.broadcasted_iota(I32, (KEY_HALF, nq), 0)
            sc = jnp.where(spos < vis_end, sc, NEG_BIG)
            sc_ref[pl.ds(r0, KEY_HALF), :] = sc
            bits = pltpu.bitcast(sc, I32)
            keys.append(bits ^ ((bits >> 31) | jnp.int32(INT_MIN)))

        kt_all = kk_ref[tile_rows(t), :]
        for h in range(A_HEADS):
            cs = slice(h * nq, (h + 1) * nq)
            lg = _dot(kt_all, qa_ref[:, cs])
            lg_ref[tile_rows(t), cs] = lg
            ml_ref[1:2, cs] = jnp.maximum(ml_ref[1:2, cs], jnp.max(lg, axis=0, keepdims=True))

        def member(parts, m):
            r = (m % 16) * 8
            return parts[m // 16][r:r + 8, :]

        planes = _bit_transpose32([member(keys, m) for m in range(WORD)])
        for i in range(WORD):
            planes_ref[i, word_rows(t), :] = planes[i]

    for_each_tile(score_tile)


def _dsa_attend(j, avt_ref, o_ref, lg_ref, sc_ref, thr_ref, pmax_ref, acc_ref, ml_ref):
    nq = DSA_QB
    n_tiles = j + 1
    tile_rows, word_rows, for_each_tile = _tile_helpers(n_tiles)
    queries = pl.ds(pl.multiple_of(j * nq, nq), nq)

    def masked_logits(t, cs, bias):
        return lg_ref[tile_rows(t), cs] + bias

    thr, thr_valid, pmax = thr_ref[0:1, :], thr_ref[1:2, :], pmax_ref[0:1, :]

    def mask_bias(t):
        s = sc_ref[tile_rows(t), :]
        pos = t * KEY_TILE + lax.broadcasted_iota(I32, (KEY_TILE, nq), 0)
        selected = (s > thr_valid) | ((s == thr) & (pos <= pmax))
        return jnp.where(selected, 0.0, NEG_BIG)

    def attend(shift_row):
        acc_ref[...] = jnp.zeros(acc_ref.shape, F32)

        def attn_tile(t):
            vt = avt_ref[:, tile_rows(t)]
            bias = mask_bias(t)
            for h in range(A_HEADS):
                cs = slice(h * nq, (h + 1) * nq)
                p = jnp.exp2(masked_logits(t, cs, bias) - ml_ref[shift_row:shift_row + 1, cs])
                acc_ref[:, cs] = acc_ref[:, cs] + _dot(vt, p.astype(BF16))

        for_each_tile(attn_tile)

    attend(1)
    underflow = jnp.min(acc_ref[A_HEAD_DIM:A_HEAD_DIM + 1, :]) < DENOM_MIN

    @pl.when(underflow)
    def _():
        ml_ref[0:1, :] = jnp.full((1, A_HEADS * nq), 0.5 * NEG_BIG, F32)

        def max_tile(t, carry):
            bias = mask_bias(t)
            for h in range(A_HEADS):
                cs = slice(h * nq, (h + 1) * nq)
                ml_ref[0:1, cs] = jnp.maximum(
                    ml_ref[0:1, cs], jnp.max(masked_logits(t, cs, bias), axis=0, keepdims=True))
            return carry

        lax.fori_loop(0, n_tiles, max_tile, 0)
        attend(0)

    out_t = acc_ref[0:A_HEAD_DIM, :] / acc_ref[A_HEAD_DIM:A_HEAD_DIM + 1, :]
    for hp in range(A_HEADS // 2):
        blk = jnp.concatenate(
            [out_t[:, (2 * hp) * nq:(2 * hp + 1) * nq], out_t[:, (2 * hp + 1) * nq:(2 * hp + 2) * nq]], axis=0)
        o_ref[queries, hp * 128:(hp + 1) * 128] = blk.T.astype(o_ref.dtype)


def _dsa(kk, avt, aqt, iqt, iwt, batch, seq, n_sel):
    t = batch * seq
    feat = lambda b: (0, b)
    return pl.pallas_call(
        functools.partial(_dsa_kernel, n_sel=n_sel, seq=seq),
        grid=(batch,),
        in_specs=[
            pl.BlockSpec((seq, 128), lambda b: (b, 0)),
            pl.BlockSpec((AV_ROWS, seq), feat),
            pl.BlockSpec((A_WIDTH, seq), feat),
            pl.BlockSpec((A_WIDTH, seq), feat),
            pl.BlockSpec((IDX_HEADS, seq), feat),
        ],
        out_specs=pl.BlockSpec((seq, A_WIDTH), lambda b: (b, 0)),
        out_shape=jax.ShapeDtypeStruct((t, A_WIDTH), BF16),
        scratch_shapes=[
            pltpu.VMEM((seq, A_HEADS * DSA_QB), F32),
            pltpu.VMEM((WORD, seq // WORD, DSA_QB), I32),
            pltpu.VMEM(((seq - 1).bit_length(), seq // WORD, DSA_QB), I32),
            pltpu.VMEM((seq, DSA_QB), F32),
            pltpu.VMEM((8, DSA_QB), F32),
            pltpu.VMEM((8, DSA_QB), I32),
            pltpu.VMEM((128, A_HEADS * DSA_QB), BF16),
            pltpu.VMEM((128, A_HEADS * DSA_QB), BF16),
            pltpu.VMEM((AV_ROWS, A_HEADS * DSA_QB), F32),
            pltpu.VMEM((8, A_HEADS * DSA_QB), F32),
        ],
        compiler_params=pltpu.CompilerParams(
            dimension_semantics=("arbitrary",), vmem_limit_bytes=VMEM_LIMIT),
        name="dsa",
    )(kk, avt, aqt, iqt, iwt)


def _log_gamma(h):
    return math.log1p(-(2.0 ** (-5.0 - h)))


def _retention_kernel(rq_ref, rkt_ref, rv_ref, rg_ref, gn_ref, o_ref,
                      state_ref, dmat_ref, dq_ref):
    nb = RET_BLOCK

    @pl.when(pl.program_id(0) == 0)
    def _():
        i = lax.broadcasted_iota(I32, (nb, nb), 0)
        jj = lax.broadcasted_iota(I32, (nb, nb), 1)
        dist = jnp.abs(i - jj).astype(F32)
        seen = (jj // CHUNK) <= (i // CHUNK)
        for h in range(R_HEADS):
            lg = _log_gamma(h)
            dmat_ref[h] = jnp.where(seen, jnp.exp(lg * dist), 0.0)
            dq_ref[h] = jnp.exp(lg * (i.astype(F32) + 1.0))

    state_ref[...] = jnp.zeros(state_ref.shape, F32)
    jrow = lax.broadcasted_iota(I32, (1, nb), 1).astype(F32)

    def block(n, carry):
        toks = pl.ds(pl.multiple_of(n * nb, nb), nb)
        for h in range(R_HEADS):
            lg = _log_gamma(h)
            q = rq_ref[toks, h * R_QK_DIM:(h + 1) * R_QK_DIM]
            kt = rkt_ref[h * R_QK_DIM:(h + 1) * R_QK_DIM, toks]
            v = rv_ref[toks, h * R_V_DIM:(h + 1) * R_V_DIM]
            state = state_ref[h]
            s = _dot(q, kt) * dmat_ref[h]
            y = _dot(s.astype(BF16), v) + _dot(q, state.astype(BF16)) * dq_ref[h]
            dk = jnp.exp(lg * (nb - 1.0 - jrow))
            kd = (kt.astype(F32) * dk).astype(BF16)
            state_ref[h] = state * math.exp(lg * nb) + _dot(kd, v)
            mean = jnp.mean(y, axis=-1, keepdims=True)
            yc = y - mean
            var = jnp.mean(yc * yc, axis=-1, keepdims=True)
            yn = yc * lax.rsqrt(var + GN_EPS) * gn_ref[:, h * R_V_DIM:(h + 1) * R_V_DIM]
            gate = rg_ref[toks, h * R_V_DIM:(h + 1) * R_V_DIM].astype(F32)
            o_ref[toks, h * R_V_DIM:(h + 1) * R_V_DIM] = (yn * (gate * jax.nn.sigmoid(gate))).astype(o_ref.dtype)
        return carry

    lax.fori_loop(0, rq_ref.shape[0] // nb, block, 0)


def _retention(rq, rkt, rv, rg, gn, batch, seq):
    t = batch * seq
    tok = lambda b: (b, 0)
    return pl.pallas_call(
        _retention_kernel,
        grid=(batch,),
        in_specs=[
            pl.BlockSpec((seq, 512), tok),
            pl.BlockSpec((512, seq), lambda b: (0, b)),
            pl.BlockSpec((seq, R_WIDTH), tok),
            pl.BlockSpec((seq, R_WIDTH), tok),
            pl.BlockSpec((1, R_WIDTH), lambda b: (0, 0)),
        ],
        out_specs=pl.BlockSpec((seq, R_WIDTH), tok),
        out_shape=jax.ShapeDtypeStruct((t, R_WIDTH), BF16),
        scratch_shapes=[
            pltpu.VMEM((R_HEADS, R_QK_DIM, R_V_DIM), F32),
            pltpu.VMEM((R_HEADS, RET_BLOCK, RET_BLOCK), F32),
            pltpu.VMEM((R_HEADS, RET_BLOCK, RET_BLOCK), F32),
        ],
        compiler_params=pltpu.CompilerParams(
            dimension_semantics=("arbitrary",), vmem_limit_bytes=VMEM_LIMIT),
        name="retention",
    )(rq, rkt, rv, rg, gn)


def _tail_kernel(h_ref, ya_ref, yr_ref, ga_ref, gb_ref, p_ref,
                 wa_ref, wb_ref, wo_ref, g2_ref, wg_ref, wu_ref, wd_ref,
                 gp_ref, wpg_ref, wpp_ref, gf_ref, o_ref):
    a = _dot(ya_ref[...], wa_ref[...])
    b = _dot(yr_ref[...], wb_ref[...])
    merged = (jax.nn.sigmoid(ga_ref[...].astype(F32)) * a
              + jax.nn.sigmoid(gb_ref[...].astype(F32)) * b)
    h = h_ref[...] + _dot(merged.astype(BF16), wo_ref[...])
    h = h + 0.5 * _swiglu(_rms(h, g2_ref[...]).astype(BF16), wg_ref, wu_ref, wd_ref)
    gate = jax.nn.sigmoid(_dot(_rms(h, gp_ref[...]).astype(BF16), wpg_ref[...]))
    emb = _dot(p_ref[...].astype(BF16), wpp_ref[...])
    o_ref[...] = _rms(h + gate * emb, gf_ref[...])


def _tail(h, ya, yr, ga, gb, p, wa, wb, wo, g2, wg, wu, wd, gp, wpg, wpp, gf, tm):
    t = h.shape[0]
    row = lambda i: (i, 0)
    return pl.pallas_call(
        _tail_kernel,
        grid=(t // tm,),
        in_specs=[
            pl.BlockSpec((tm, D_MODEL), row),
            pl.BlockSpec((tm, A_WIDTH), row),
            pl.BlockSpec((tm, R_WIDTH), row),
            pl.BlockSpec((tm, D_MODEL), row),
            pl.BlockSpec((tm, D_MODEL), row),
            pl.BlockSpec((tm, PLE_DIM), row),
            _const_spec((A_WIDTH, D_MODEL)),
            _const_spec((R_WIDTH, D_MODEL)),
            _const_spec((D_MODEL, D_MODEL)),
            _const_spec((1, D_MODEL)),
            _const_spec((D_MODEL, D_FF)),
            _const_spec((D_MODEL, D_FF)),
            _const_spec((D_FF, D_MODEL)),
            _const_spec((1, D_MODEL)),
            _const_spec((D_MODEL, D_MODEL)),
            _const_spec((PLE_DIM, D_MODEL)),
            _const_spec((1, D_MODEL)),
        ],
        out_specs=pl.BlockSpec((tm, D_MODEL), row),
        out_shape=jax.ShapeDtypeStruct((t, D_MODEL), F32),
        compiler_params=pltpu.CompilerParams(
            dimension_semantics=("parallel",), vmem_limit_bytes=VMEM_LIMIT),
        name="tail",
    )(h, ya, yr, ga, gb, p, wa, wb, wo, g2, wg, wu, wd, gp, wpg, wpp, gf)


def _token_tile(t):
    return 512 if t % 512 == 0 else 256


def kernel(x, p, positions, ffn1_norm, ffn1_w_gate, ffn1_w_up, ffn1_w_down, mix_norm, w_in, ret_gn, w_branch_a, w_branch_b, w_out, ffn2_norm, ffn2_w_gate, ffn2_w_up, ffn2_w_down, ple_norm, w_ple_gate, w_ple_proj, final_norm):
    batch, seq, _ = x.shape
    depth = p.shape[0]
    t = batch * seq
    tm = _token_tile(t)
    n_sel = min(TOPK_MAX, seq // 4)
    assert depth == 1, "the final norm is fused into the per-layer embedding step"
    assert seq % RET_BLOCK == 0 and seq % KEY_TILE == 0 and seq % DSA_QB == 0
    assert n_sel <= KEY_TILE and DSA_QB == KEY_TILE and KEY_TILE == 8 * WORD

    h = x.reshape(t, D_MODEL)
    pos3 = positions.reshape(t // tm, 1, tm)
    vec = lambda g: g.reshape(1, -1).astype(F32)

    for i in range(depth):
        w = w_in[i].astype(BF16)
        wt = jnp.concatenate([
            w[:, _OFF_AQ:_OFF_AQ + A_WIDTH], w[:, _OFF_IQ:_OFF_IQ + IDX_HEADS * IDX_DIM],
            w[:, _OFF_RQ:_OFF_RQ + 512], w[:, _OFF_RK:_OFF_RK + 512],
            w[:, _OFF_AK:_OFF_AK + 64], w[:, _OFF_IK:_OFF_IK + 64],
            w[:, _OFF_AV:_OFF_AV + 64], w[:, _OFF_IW:_OFF_IW + IDX_HEADS]], axis=1).T
        wn = w[:, _OFF_RV:]

        h = _ffn(h, vec(ffn1_norm[i]), ffn1_w_gate[i].astype(BF16), ffn1_w_up[i].astype(BF16),
                 ffn1_w_down[i].astype(BF16), tm)
        aqt, iqt, rq, rkt, kk, avt, iwt, rv, rg, ga, gb = _inproj(h, vec(mix_norm[i]), pos3, wt, wn, tm)
        ya = _dsa(kk, avt, aqt, iqt, iwt, batch, seq, n_sel)
        yr = _retention(rq, rkt, rv, rg, vec(ret_gn[i]), batch, seq)
        h = _tail(h, ya, yr, ga, gb, p[i].reshape(t, PLE_DIM),
                  w_branch_a[i].astype(BF16), w_branch_b[i].astype(BF16), w_out[i].astype(BF16),
                  vec(ffn2_norm[i]), ffn2_w_gate[i].astype(BF16), ffn2_w_up[i].astype(BF16),
                  ffn2_w_down[i].astype(BF16),
                  vec(ple_norm[i]), w_ple_gate[i].astype(BF16), w_ple_proj[i].astype(BF16),
                  vec(final_norm), tm)
    return h.reshape(batch, seq, D_MODEL)
```
